```python
import jax, jax.numpy as jnp
from jax import lax
import numpy as np

D_MODEL = 2048
BATCH = 16
SEQ = 2048
DEPTH = 4

CTX_LEN = 256
GRID_W = 64
N_MIXERS = 2
D_FF = ((8 * D_MODEL + 3 * 256 - 1) // (3 * 256)) * 256
CONV_WIDTH = 31
N_HEADS_M = max(4, D_MODEL // 512)
DQK = D_MODEL // 2
DV = D_MODEL
HEAD_QK = DQK // N_HEADS_M
HEAD_V = DV // N_HEADS_M
CHUNK = 64
GATE_CAP = 15.0
EPS = 1e-6
N_CONV_LAYERS = (DEPTH + 1) // 2
N_MLSTM_LAYERS = DEPTH // 2
W_IN_COLS = 2 * DQK + 2 * DV + 4 * N_HEADS_M

kernel_name = "hybrid_conv_mlstm_prefix_dit"


def rmsnorm(x, g):
    xf = x.astype(jnp.float32)
    y = xf * lax.rsqrt(jnp.mean(xf * xf, axis=-1, keepdims=True) + EPS)
    return (y * g.astype(jnp.float32)).astype(x.dtype)


def layernorm(x, g, b):
    xf = x.astype(jnp.float32)
    mu = jnp.mean(xf, axis=-1, keepdims=True)
    var = jnp.mean(jnp.square(xf - mu), axis=-1, keepdims=True)
    y = (xf - mu) * lax.rsqrt(var + EPS)
    return (y * g.astype(jnp.float32) + b.astype(jnp.float32)).astype(x.dtype)


def modulation(cvec, w, b):
    m = jax.nn.silu(cvec) @ w + b
    parts = jnp.split(m, 6, axis=-1)
    if cvec.ndim == 2:
        parts = [p[:, None, :] for p in parts]
    return parts


def modulate(h, shift, scale):
    return h * (1 + scale) + shift


def swiglu_ffn(h, w_gu, w_down):
    g, u = jnp.split(h @ w_gu, 2, axis=-1)
    return (jax.nn.silu(g) * u) @ w_down


def depthwise_conv(x, w, b):
    k = w.shape[0]
    out = lax.conv_general_dilated(
        x, w[:, None, :].astype(x.dtype), window_strides=(1,),
        padding=[(k // 2, k // 2)], dimension_numbers=('NWC', 'WIO', 'NWC'),
        feature_group_count=x.shape[-1])
    return out + b


def conformer_conv(h, on_grid, w_pw1, b_pw1, w_dw, b_dw, ln_g, ln_b, w_pw2, b_pw2):
    B, L, _ = h.shape
    a = h @ w_pw1 + b_pw1
    d = a.shape[-1] // 2
    u = a[..., :d] * jax.nn.sigmoid(a[..., d:])
    if on_grid:
        rows = L // GRID_W
        half = d // 2
        ur = u[..., :half].reshape(B * rows, GRID_W, half)
        ur = depthwise_conv(ur, w_dw[:, :half], b_dw[:half]).reshape(B, L, half)
        uc = u[..., half:].reshape(B, rows, GRID_W, d - half).transpose(0, 2, 1, 3)
        uc = uc.reshape(B * GRID_W, rows, d - half)
        uc = depthwise_conv(uc, w_dw[:, half:], b_dw[half:])
        uc = uc.reshape(B, GRID_W, rows, d - half).transpose(0, 2, 1, 3).reshape(B, L, d - half)
        u = jnp.concatenate([ur, uc], axis=-1)
    else:
        u = depthwise_conv(u, w_dw, b_dw)
    u = layernorm(u, ln_g, ln_b)
    return jax.nn.silu(u) @ w_pw2 + b_pw2


def to_heads(a, d):
    B, L, _ = a.shape
    return a.reshape(B, L, -1, d).transpose(0, 2, 1, 3).astype(jnp.float32)


def mlstm_project(h, w_in, b_gates):
    B, L, _ = h.shape
    proj = h @ w_in
    q, k, v, o, g = jnp.split(proj, [DQK, 2 * DQK, 2 * DQK + DV, 2 * DQK + 2 * DV], axis=-1)
    q = to_heads(q, HEAD_QK)
    k = to_heads(k, HEAD_QK) * (HEAD_QK ** -0.5)
    v = to_heads(v, HEAD_V)
    g = g.astype(jnp.float32) + b_gates.astype(jnp.float32)
    g = GATE_CAP * jnp.tanh(g / GATE_CAP)
    g = g.reshape(B, L, 4, N_HEADS_M).transpose(2, 0, 3, 1)
    fwd = (g[0], jax.nn.log_sigmoid(g[1]))
    bwd = (g[2], jax.nn.log_sigmoid(g[3]))
    return q, k, v, o, fwd, bwd


def mlstm_scan(q, k, v, li, lf, state):
    B, H, L, _ = q.shape
    nc = L // CHUNK
    tril = jnp.tril(jnp.ones((CHUNK, CHUNK), dtype=bool))

    def chunks(a):
        a = a.reshape(a.shape[:2] + (nc, CHUNK) + a.shape[3:])
        return jnp.moveaxis(a, 2, 0)

    def step(carry, inp):
        C, n, m = carry
        qc, kc, vc, lic, lfc = inp
        b = jnp.cumsum(lfc, axis=-1)
        a = b + m[..., None]
        dmat = b[..., :, None] - b[..., None, :] + lic[..., None, :]
        dmat = jnp.where(tril, dmat, -jnp.inf)
        mt = jnp.maximum(a, jnp.max(dmat, axis=-1))
        w_inter = jnp.exp(a - mt)
        s = jnp.einsum('bhtd,bhsd->bhts', qc, kc) * jnp.exp(dmat - mt[..., None])
        num = (w_inter[..., None] * jnp.einsum('bhtd,bhde->bhte', qc, C)
               + jnp.einsum('bhts,bhse->bhte', s, vc))
        den = w_inter * jnp.einsum('bhtd,bhd->bht', qc, n) + jnp.sum(s, axis=-1)
        h = num / jnp.maximum(jnp.abs(den), jnp.exp(-mt))[..., None]
        bT = b[..., -1]
        gl = bT[..., None] - b + lic
        m_new = jnp.maximum(bT + m, jnp.max(gl, axis=-1))
        decay = jnp.exp(bT + m - m_new)
        kw = kc * jnp.exp(gl - m_new[..., None])[..., None]
        C_new = decay[..., None, None] * C + jnp.einsum('bhsd,bhse->bhde', kw, vc)
        n_new = decay[..., None] * n + jnp.sum(kw, axis=2)
        return (C_new, n_new, m_new), h

    state, hs = lax.scan(step, state, (chunks(q), chunks(k), chunks(v), chunks(li), chunks(lf)))
    h = jnp.moveaxis(hs, 0, 2).reshape(B, H, L, v.shape[-1])
    return h, state


def zero_state(B):
    return (jnp.zeros((B, N_HEADS_M, HEAD_QK, HEAD_V), jnp.float32),
            jnp.zeros((B, N_HEADS_M, HEAD_QK), jnp.float32),
            jnp.zeros((B, N_HEADS_M), jnp.float32))


def mlstm_bidir(q, k, v, fwd, bwd, init_f, init_b):
    h_f, st_f = mlstm_scan(q, k, v, fwd[0], fwd[1], init_f)
    flip = lambda a: jnp.flip(a, axis=2)
    h_b, st_b = mlstm_scan(flip(q), flip(k), flip(v), flip(bwd[0]), flip(bwd[1]), init_b)
    return h_f + flip(h_b), st_f, st_b


def mlstm_output(hs, o, hn_g, w_out):
    B, H, L, dv = hs.shape
    hn = hs * lax.rsqrt(jnp.mean(hs * hs, axis=-1, keepdims=True) + EPS)
    hn = hn.transpose(0, 2, 1, 3).reshape(B, L, H * dv).astype(o.dtype)
    return (hn * hn_g * jax.nn.sigmoid(o)) @ w_out


def setup_inputs(seed: int = 0) -> dict:
    key = jax.random.key(seed)
    ks = jax.random.split(key, 24)
    f32 = jnp.float32

    def nrm(k, shape, scale):
        return jax.random.normal(k, shape, f32) * scale

    D, F, H = D_MODEL, D_FF, N_HEADS_M
    gate_base = jnp.tile(jnp.concatenate([jnp.zeros((H,), f32), jnp.full((H,), 3.0, f32)]), 2)
    return {
        "x": nrm(ks[0], (BATCH, SEQ, D), 1.0),
        "c": nrm(ks[1], (BATCH, D), 1.0),
        "ctx": nrm(ks[2], (BATCH, CTX_LEN, D), 1.0),
        "c_ctx": nrm(ks[3], (D,), 1.0),
        "norm1_g": 1.0 + nrm(ks[4], (DEPTH, D), 0.05),
        "norm2_g": 1.0 + nrm(ks[5], (DEPTH, D), 0.05),
        "w_mod": nrm(ks[6], (DEPTH, D, 6 * D), 0.5 * D ** -0.5),
        "b_mod": nrm(ks[7], (DEPTH, 6 * D), 0.02),
        "w_gu": nrm(ks[8], (DEPTH, D, 2 * F), D ** -0.5),
        "w_down": nrm(ks[9], (DEPTH, F, D), F ** -0.5),
        "conv_w_pw1": nrm(ks[10], (N_CONV_LAYERS, D, 2 * D), D ** -0.5),
        "conv_b_pw1": nrm(ks[11], (N_CONV_LAYERS, 2 * D), 0.02),
        "conv_w_dw": nrm(ks[12], (N_CONV_LAYERS, CONV_WIDTH, D), CONV_WIDTH ** -0.5),
        "conv_b_dw": nrm(ks[13], (N_CONV_LAYERS, D), 0.02),
        "conv_ln_g": 1.0 + nrm(ks[14], (N_CONV_LAYERS, D), 0.05),
        "conv_ln_b": nrm(ks[15], (N_CONV_LAYERS, D), 0.02),
        "conv_w_pw2": nrm(ks[16], (N_CONV_LAYERS, D, D), D ** -0.5),
        "conv_b_pw2": nrm(ks[17], (N_CONV_LAYERS, D), 0.02),
        "m_w_in": nrm(ks[18], (N_MLSTM_LAYERS, D, W_IN_COLS), D ** -0.5),
        "m_b_gates": gate_base[None, :] + nrm(ks[19], (N_MLSTM_LAYERS, 4 * H), 0.1),
        "m_hn_g": 1.0 + nrm(ks[20], (N_MLSTM_LAYERS, D), 0.05),
        "m_w_out": nrm(ks[21], (N_MLSTM_LAYERS, D, D), D ** -0.5),
        "final_g": 1.0 + nrm(ks[22], (D,), 0.05),
    }


def reference(x, c, ctx, c_ctx, norm1_g, norm2_g, w_mod, b_mod, w_gu, w_down,
              conv_w_pw1, conv_b_pw1, conv_w_dw, conv_b_dw, conv_ln_g, conv_ln_b,
              conv_w_pw2, conv_b_pw2, m_w_in, m_b_gates, m_hn_g, m_w_out, final_g):
    B = x.shape[0]
    for i in range(DEPTH):
        last = i == DEPTH - 1
        j = i // N_MIXERS
        sx1, cx1, gx1, sx2, cx2, gx2 = modulation(c, w_mod[i], b_mod[i])
        sc1, cc1, gc1, sc2, cc2, gc2 = modulation(c_ctx, w_mod[i], b_mod[i])
        hx = modulate(rmsnorm(x, norm1_g[i]), sx1, cx1)
        hc = modulate(rmsnorm(ctx, norm1_g[i]), sc1, cc1)
        if i % N_MIXERS == 0:
            conv_p = (conv_w_pw1[j], conv_b_pw1[j], conv_w_dw[j], conv_b_dw[j],
                      conv_ln_g[j], conv_ln_b[j], conv_w_pw2[j], conv_b_pw2[j])
            yx = conformer_conv(hx, True, *conv_p)
            if not last:
                yc = conformer_conv(hc, False, *conv_p)
        else:
            qc, kc, vc, oc, fc, bc = mlstm_project(hc, m_w_in[j], m_b_gates[j])
            hsc, st_f, st_b = mlstm_bidir(qc, kc, vc, fc, bc, zero_state(B), zero_state(B))
            qx, kx, vx, ox, fx, bx = mlstm_project(hx, m_w_in[j], m_b_gates[j])
            hsx, _, _ = mlstm_bidir(qx, kx, vx, fx, bx, st_f, st_b)
            yx = mlstm_output(hsx, ox, m_hn_g[j], m_w_out[j])
            if not last:
                yc = mlstm_output(hsc, oc, m_hn_g[j], m_w_out[j])
        x = x + gx1 * yx
        x = x + gx2 * swiglu_ffn(modulate(rmsnorm(x, norm2_g[i]), sx2, cx2), w_gu[i], w_down[i])
        if not last:
            ctx = ctx + gc1 * yc
            ctx = ctx + gc2 * swiglu_ffn(modulate(rmsnorm(ctx, norm2_g[i]), sc2, cc2), w_gu[i], w_down[i])
    return rmsnorm(x, final_g)
```

```python
import functools

import jax
import jax.numpy as jnp
from jax import lax
from jax.experimental import pallas as pl
from jax.experimental.pallas import tpu as pltpu

EPS = 1e-6
GATE_CAP = 15.0
GRID_W = 64
LANES = 128
VMEM_LIMIT = 56 * 1024 * 1024
MOD_ROWS = 32

_BF = jnp.bfloat16
_F32 = jnp.float32


def _tile(dim, pref):
    t = min(dim, pref)
    while dim % t:
        t //= 2
    return t


def _params(*sem):
    return pltpu.CompilerParams(dimension_semantics=sem, vmem_limit_bytes=VMEM_LIMIT)


def _sigmoid(v):
    return 1.0 / (1.0 + jnp.exp(-v))


def _dot(a, b):
    return jnp.dot(a, b, preferred_element_type=_F32)


def _rms_mod(x, g, shift, scale):
    y = x * lax.rsqrt(jnp.mean(x * x, axis=-1, keepdims=True) + EPS)
    return (y * g) * (1.0 + scale) + shift


def _mod_kernel(cc_ref, w_ref, b_ref, o_ref):
    cc = cc_ref[...]
    s = (cc * _sigmoid(cc)).astype(_BF)
    o_ref[...] = _dot(s, w_ref[...].astype(_BF)) + b_ref[...]


def _modulation(cc, w_mod, b_mod):
    depth, d, n = w_mod.shape
    tn = _tile(n, 1024)
    return pl.pallas_call(
        _mod_kernel,
        grid=(depth, n // tn),
        in_specs=[
            pl.BlockSpec((MOD_ROWS, d), lambda l, j: (0, 0)),
            pl.BlockSpec((None, d, tn), lambda l, j: (l, 0, j)),
            pl.BlockSpec((None, 1, tn), lambda l, j: (l, 0, j)),
        ],
        out_specs=pl.BlockSpec((None, MOD_ROWS, tn), lambda l, j: (l, 0, j)),
        out_shape=jax.ShapeDtypeStruct((depth, MOD_ROWS, n), _F32),
        compiler_params=_params("parallel", "parallel"),
        name="modulation",
    )(cc, w_mod, b_mod.reshape(depth, 1, n))


class _Layout:
    def __init__(self, batch, seq, ctx_len, d, tm_pref):
        self.batch, self.seq, self.ctx_len, self.d = batch, seq, ctx_len, d
        self.nx = batch * seq
        self.nc = batch * ctx_len
        self.n = self.nx + self.nc
        self.tm = _tile(ctx_len * batch, _tile(seq, tm_pref))
        assert seq % self.tm == 0 and self.nc % self.tm == 0
        self.tiles_per_sample = seq // self.tm
        self.x_tiles = self.nx // self.tm
        self.all_tiles = self.n // self.tm

    def sample(self, i):
        return jnp.minimum(i // self.tiles_per_sample, self.batch)


def _mod_spec(lay, layer, part, tn=None):
    if tn is None:
        return pl.BlockSpec((None, None, None, 1, lay.d),
                            lambda i, j: (layer, lay.sample(i), part, 0, 0))
    return pl.BlockSpec((None, None, None, 1, tn),
                        lambda i, j: (layer, lay.sample(i), part, 0, j))


def _pw1_glu_kernel(x_ref, g_ref, sh_ref, sc_ref, wa_ref, wb_ref, ba_ref, bb_ref, u_ref, h_ref):
    @pl.when(pl.program_id(1) == 0)
    def _():
        h_ref[...] = _rms_mod(x_ref[...], g_ref[...], sh_ref[...], sc_ref[...]).astype(_BF)

    h = h_ref[...]
    a = _dot(h, wa_ref[...]) + ba_ref[...]
    b = _dot(h, wb_ref[...]) + bb_ref[...]
    u_ref[...] = a * _sigmoid(b)


def _pw1_glu(lay, xs, mods, layer, g1, w_pw1, b_pw1):
    d = lay.d
    tm, tn = lay.tm, _tile(d, 512)
    nj = d // tn
    return pl.pallas_call(
        _pw1_glu_kernel,
        grid=(lay.all_tiles, nj),
        in_specs=[
            pl.BlockSpec((tm, d), lambda i, j: (i, 0)),
            pl.BlockSpec((1, d), lambda i, j: (0, 0)),
            _mod_spec(lay, layer, 0),
            _mod_spec(lay, layer, 1),
            pl.BlockSpec((d, tn), lambda i, j: (0, j)),
            pl.BlockSpec((d, tn), lambda i, j: (0, nj + j)),
            pl.BlockSpec((1, tn), lambda i, j: (0, j)),
            pl.BlockSpec((1, tn), lambda i, j: (0, nj + j)),
        ],
        out_specs=pl.BlockSpec((tm, tn), lambda i, j: (i, j)),
        out_shape=jax.ShapeDtypeStruct((lay.n, d), _F32),
        scratch_shapes=[pltpu.VMEM((tm, d), _BF)],
        compiler_params=_params("parallel", "arbitrary"),
        name="pw1_glu",
    )(xs, g1.reshape(1, d), mods, mods, w_pw1, w_pw1, b_pw1.reshape(1, 2 * d), b_pw1.reshape(1, 2 * d))


def _dwconv_kernel(u_ref, w_ref, b_ref, o_ref, padh_ref, padv_ref, padc_ref, *,
                   n_x_blocks, n_h_tiles, taps, rows, ctx_len):
    i = pl.program_id(0)
    c = pl.program_id(1)
    half = taps // 2
    lpad = 16
    tc = u_ref.shape[-1]
    n_ctx = rows * GRID_W // ctx_len

    def conv_rows(pad_ref, n_rows, row_len, out_view):
        def body(r, carry):
            for l0 in range(0, tc, LANES):
                acc = jnp.zeros((row_len, LANES), _F32)
                for k in range(taps):
                    start = lpad - half + k
                    acc = acc + pad_ref[r, pl.ds(start, row_len), pl.ds(l0, LANES)] * w_ref[pl.ds(k, 1), pl.ds(l0, LANES)]
                out_view(r, l0, acc + b_ref[:, pl.ds(l0, LANES)])
            return carry
        lax.fori_loop(0, n_rows, body, 0)

    @pl.when(jnp.logical_and(i < n_x_blocks, c < n_h_tiles))
    def _():
        zeros = jnp.zeros((rows, lpad, tc), _F32)
        padh_ref[:, pl.ds(0, lpad), :] = zeros
        padh_ref[:, pl.ds(lpad + GRID_W, lpad), :] = zeros
        padh_ref[:, pl.ds(lpad, GRID_W), :] = u_ref[...]

        def out_view(r, l0, val):
            o_ref[r, :, pl.ds(l0, LANES)] = val
        conv_rows(padh_ref, rows, GRID_W, out_view)

    @pl.when(jnp.logical_and(i < n_x_blocks, c >= n_h_tiles))
    def _():
        zeros = jnp.zeros((half, GRID_W, tc), _F32)
        padv_ref[pl.ds(0, half)] = zeros
        padv_ref[pl.ds(half + rows, half)] = zeros
        padv_ref[pl.ds(half, rows)] = u_ref[...]

        def body(r, carry):
            for l0 in range(0, tc, LANES):
                acc = jnp.zeros((GRID_W, LANES), _F32)
                for k in range(taps):
                    acc = acc + padv_ref[r + k, :, pl.ds(l0, LANES)] * w_ref[pl.ds(k, 1), pl.ds(l0, LANES)]
                o_ref[r, :, pl.ds(l0, LANES)] = acc + b_ref[:, pl.ds(l0, LANES)]
            return carry
        lax.fori_loop(0, rows, body, 0)

    @pl.when(i >= n_x_blocks)
    def _():
        zeros = jnp.zeros((n_ctx, lpad, tc), _F32)
        padc_ref[:, pl.ds(0, lpad), :] = zeros
        padc_ref[:, pl.ds(lpad + ctx_len, lpad), :] = zeros
        padc_ref[:, pl.ds(lpad, ctx_len), :] = u_ref[...].reshape(n_ctx, ctx_len, tc)
        per = ctx_len // GRID_W

        def out_view(r, l0, val):
            for p in range(per):
                o_ref[r * per + p, :, pl.ds(l0, LANES)] = val[p * GRID_W:(p + 1) * GRID_W]
        conv_rows(padc_ref, n_ctx, ctx_len, out_view)


def _dwconv(lay, u, w_dw, b_dw):
    d = lay.d
    taps = w_dw.shape[0]
    rows = lay.seq // GRID_W
    assert lay.seq % GRID_W == 0 and lay.ctx_len % GRID_W == 0
    assert lay.seq % lay.ctx_len == 0 and lay.nc % lay.seq == 0 and taps // 2 <= 16
    tc = _tile(d // 2, 256)
    n_blocks = lay.n // lay.seq
    n_ctx = lay.seq // lay.ctx_len
    kern = functools.partial(_dwconv_kernel, n_x_blocks=lay.batch, n_h_tiles=(d // 2) // tc,
                             taps=taps, rows=rows, ctx_len=lay.ctx_len)
    out = pl.pallas_call(
        kern,
        grid=(n_blocks, d // tc),
        in_specs=[
            pl.BlockSpec((rows, GRID_W, tc), lambda i, c: (i, 0, c)),
            pl.BlockSpec((taps, tc), lambda i, c: (0, c)),
            pl.BlockSpec((1, tc), lambda i, c: (0, c)),
        ],
        out_specs=pl.BlockSpec((rows, GRID_W, tc), lambda i, c: (i, 0, c)),
        out_shape=jax.ShapeDtypeStruct((lay.n // GRID_W, GRID_W, d), _F32),
        scratch_shapes=[
            pltpu.VMEM((rows, GRID_W + 32, tc), _F32),
            pltpu.VMEM((rows + 2 * (taps // 2), GRID_W, tc), _F32),
            pltpu.VMEM((n_ctx, lay.ctx_len + 32, tc), _F32),
        ],
        compiler_params=_params("parallel", "parallel"),
        name="dwconv",
    )(u.reshape(lay.n // GRID_W, GRID_W, d), w_dw, b_dw.reshape(1, d))
    return out.reshape(lay.n, d)


def _pw2_res_kernel(v_ref, lg_ref, lb_ref, w_ref, b_ref, x_ref, gate_ref, o_ref, z_ref):
    @pl.when(pl.program_id(1) == 0)
    def _():
        v = v_ref[...]
        mu = jnp.mean(v, axis=-1, keepdims=True)
        vc = v - mu
        var = jnp.mean(vc * vc, axis=-1, keepdims=True)
        y = vc * lax.rsqrt(var + EPS) * lg_ref[...] + lb_ref[...]
        z_ref[...] = (y * _sigmoid(y)).astype(_BF)

    y = _dot(z_ref[...], w_ref[...]) + b_ref[...]
    o_ref[...] = x_ref[...] + gate_ref[...] * y


def _pw2_res(lay, v, xs, mods, layer, ln_g, ln_b, w_pw2, b_pw2, n_tiles):
    d = lay.d
    tm, tn = lay.tm, _tile(d, 512)
    return pl.pallas_call(
        _pw2_res_kernel,
        grid=(n_tiles, d // tn),
        in_specs=[
            pl.BlockSpec((tm, d), lambda i, j: (i, 0)),
            pl.BlockSpec((1, d), lambda i, j: (0, 0)),
            pl.BlockSpec((1, d), lambda i, j: (0, 0)),
            pl.BlockSpec((d, tn), lambda i, j: (0, j)),
            pl.BlockSpec((1, tn), lambda i, j: (0, j)),
            pl.BlockSpec((tm, tn), lambda i, j: (i, j)),
            _mod_spec(lay, layer, 2, tn),
        ],
        out_specs=pl.BlockSpec((tm, tn), lambda i, j: (i, j)),
        out_shape=jax.ShapeDtypeStruct((n_tiles * tm, d), _F32),
        scratch_shapes=[pltpu.VMEM((tm, d), _BF)],
        compiler_params=_params("parallel", "arbitrary"),
        name="pw2_res",
    )(v, ln_g.reshape(1, d), ln_b.reshape(1, d), w_pw2, b_pw2.reshape(1, d), xs, mods)


def _ffn_kernel(x_ref, g_ref, sh_ref, sc_ref, gate_ref, wg_ref, wu_ref, wd_ref, fg_ref, o_ref,
                h_ref, acc_ref, *, final_norm):
    f = pl.program_id(1)

    @pl.when(f == 0)
    def _():
        h_ref[...] = _rms_mod(x_ref[...], g_ref[...], sh_ref[...], sc_ref[...]).astype(_BF)
        acc_ref[...] = jnp.zeros_like(acc_ref)

    h = h_ref[...]
    g = _dot(h, wg_ref[...])
    u = _dot(h, wu_ref[...])
    act = (g * _sigmoid(g) * u).astype(_BF)
    acc_ref[...] += _dot(act, wd_ref[...])

    @pl.when(f == pl.num_programs(1) - 1)
    def _():
        y = x_ref[...] + gate_ref[...] * acc_ref[...]
        if final_norm:
            y = y * lax.rsqrt(jnp.mean(y * y, axis=-1, keepdims=True) + EPS) * fg_ref[...]
        o_ref[...] = y


def _ffn(lay, xs, mods, layer, g2, w_gu, w_down, final_g, n_tiles, final_norm):
    d = lay.d
    ff = w_down.shape[0]
    tm, tf = lay.tm, _tile(ff, 512)
    nf = ff // tf
    return pl.pallas_call(
        functools.partial(_ffn_kernel, final_norm=final_norm),
        grid=(n_tiles, nf),
        in_specs=[
            pl.BlockSpec((tm, d), lambda i, f: (i, 0)),
            pl.BlockSpec((1, d), lambda i, f: (0, 0)),
            _mod_spec(lay, layer, 3),
            _mod_spec(lay, layer, 4),
            _mod_spec(lay, layer, 5),
            pl.BlockSpec((d, tf), lambda i, f: (0, f)),
            pl.BlockSpec((d, tf), lambda i, f: (0, nf + f)),
            pl.BlockSpec((tf, d), lambda i, f: (f, 0)),
            pl.BlockSpec((1, d), lambda i, f: (0, 0)),
        ],
        out_specs=pl.BlockSpec((tm, d), lambda i, f: (i, 0)),
        out_shape=jax.ShapeDtypeStruct((n_tiles * tm, d), _F32),
        scratch_shapes=[pltpu.VMEM((tm, d), _BF), pltpu.VMEM((tm, d), _F32)],
        compiler_params=_params("parallel", "arbitrary"),
        name="ffn",
    )(xs, g2.reshape(1, d), mods, mods, mods, w_gu, w_gu, w_down, final_g.reshape(1, d))


def _min_proj_kernel(x_ref, g_ref, sh_ref, sc_ref, w_ref, wg_ref, p_ref, gt_ref, h_ref):
    @pl.when(pl.program_id(1) == 0)
    def _():
        h_ref[...] = _rms_mod(x_ref[...], g_ref[...], sh_ref[...], sc_ref[...]).astype(_BF)
        gt_ref[...] = _dot(h_ref[...], wg_ref[...])

    p_ref[...] = _dot(h_ref[...], w_ref[...]).astype(_BF)


def _min_proj(lay, xs, mods, layer, g1, w_main, w_gates):
    d = lay.d
    n_main = w_main.shape[1]
    tm, tn = lay.tm, _tile(n_main, 512)
    return pl.pallas_call(
        _min_proj_kernel,
        grid=(lay.all_tiles, n_main // tn),
        in_specs=[
            pl.BlockSpec((tm, d), lambda i, j: (i, 0)),
            pl.BlockSpec((1, d), lambda i, j: (0, 0)),
            _mod_spec(lay, layer, 0),
            _mod_spec(lay, layer, 1),
            pl.BlockSpec((d, tn), lambda i, j: (0, j)),
            pl.BlockSpec((d, LANES), lambda i, j: (0, 0)),
        ],
        out_specs=[
            pl.BlockSpec((tm, tn), lambda i, j: (i, j)),
            pl.BlockSpec((tm, LANES), lambda i, j: (i, 0)),
        ],
        out_shape=[
            jax.ShapeDtypeStruct((lay.n, n_main), _BF),
            jax.ShapeDtypeStruct((lay.n, LANES), _F32),
        ],
        scratch_shapes=[pltpu.VMEM((tm, d), _BF)],
        compiler_params=_params("parallel", "arbitrary"),
        name="mlstm_in_proj",
    )(xs, g1.reshape(1, d), mods, mods, w_main, w_gates)


def _log_sigmoid(v):
    return jnp.minimum(v, 0.0) - jnp.log(1.0 + jnp.exp(-jnp.abs(v)))


def _gate_prep_kernel(g_ref, b_ref, col_ref, row_ref, *, heads, chunk):
    rows = g_ref.shape[0]
    g = g_ref[...] + b_ref[...]
    g = GATE_CAP * jnp.tanh(g / GATE_CAP)
    lane = lax.broadcasted_iota(jnp.int32, g.shape, 1)
    is_f = (lane // heads) % 2 == 1
    vals = jnp.where(is_f, _log_sigmoid(g), g)
    t = lax.broadcasted_iota(jnp.int32, g.shape, 0) % chunk
    pre = vals
    suf = vals
    shift = 1
    while shift < chunk:
        pre = pre + jnp.where(t >= shift, pltpu.roll(pre, shift, 0), 0.0)
        suf = suf + jnp.where(t + shift < chunk, pltpu.roll(suf, rows - shift, 0), 0.0)
        shift *= 2
    lane_o = lax.broadcasted_iota(jnp.int32, (rows, LANES), 1)
    for h in range(heads):
        li_f = vals[:, h:h + 1]
        b_f = pre[:, heads + h:heads + h + 1]
        li_b = vals[:, 2 * heads + h:2 * heads + h + 1]
        b_b = suf[:, 3 * heads + h:3 * heads + h + 1]
        blk = jnp.where(lane_o == 0, li_f, jnp.where(lane_o == 1, b_f, jnp.where(lane_o == 2, li_b, b_b)))
        col_ref[:, pl.ds(h * LANES, LANES)] = blk
        row_ref[pl.ds(h * 8, 8), :] = blk.T[0:8, :]


def _gate_prep(lay, gates, b_gates, heads, chunk):
    n = lay.n
    tr = lay.tm
    assert tr % chunk == 0
    bias = jnp.zeros((1, LANES), _F32).at[0, :4 * heads].set(b_gates)
    return pl.pallas_call(
        functools.partial(_gate_prep_kernel, heads=heads, chunk=chunk),
        grid=(n // tr,),
        in_specs=[
            pl.BlockSpec((tr, LANES), lambda i: (i, 0)),
            pl.BlockSpec((1, LANES), lambda i: (0, 0)),
        ],
        out_specs=[
            pl.BlockSpec((tr, heads * LANES), lambda i: (i, 0)),
            pl.BlockSpec((heads * 8, tr), lambda i: (0, i)),
        ],
        out_shape=[
            jax.ShapeDtypeStruct((n, heads * LANES), _F32),
            jax.ShapeDtypeStruct((heads * 8, n), _F32),
        ],
        compiler_params=_params("parallel"),
        name="mlstm_gate_prep",
    )(gates, bias)


def _scan_kernel(qx_ref, kx_ref, vx_ref, ox_ref, gcx_ref, grx_ref,
                 qc_ref, kc_ref, vc_ref, oc_ref, gcc_ref, grc_ref, hg_ref,
                 zx_ref, zc_ref,
                 cf_ref, nf_ref, mf_ref, cb_ref, nb_ref, mb_ref, hx_ref, hc_ref, *, chunk, scale):
    T = chunk
    row_i = lax.broadcasted_iota(jnp.int32, (T, T), 0)
    col_i = lax.broadcasted_iota(jnp.int32, (T, T), 1)

    for ref in (cf_ref, nf_ref, mf_ref, cb_ref, nb_ref, mb_ref):
        ref[...] = jnp.zeros_like(ref)

    def step(q_ref, k_ref, v_ref, gc_ref, gr_ref, r0, c_ref, n_ref, m_ref, backward):
        rows = pl.ds(r0, T)
        q = q_ref[rows, :]
        k = k_ref[rows, :]
        v = v_ref[rows, :]
        gcb = gc_ref[rows, :]
        grb = gr_ref[:, rows]
        o = 2 if backward else 0
        li_c, b_c = gcb[:, o:o + 1], gcb[:, o + 1:o + 2]
        li_r, b_r = grb[o:o + 1, :], grb[o + 1:o + 2, :]
        m = m_ref[0:1, 0:1]
        a = b_c + m
        dmat = b_c - b_r + li_r
        dmat = jnp.where((row_i <= col_i) if backward else (row_i >= col_i), dmat, -jnp.inf)
        mt = jnp.maximum(a, jnp.max(dmat, axis=1, keepdims=True))
        w_inter = jnp.exp(a - mt)
        qk = lax.dot_general(q, k, (((1,), (1,)), ((), ())), preferred_element_type=_F32)
        s = qk * scale * jnp.exp(dmat - mt)
        c_old = c_ref[...]
        n_old = n_ref[...]
        num = w_inter * _dot(q, c_old.astype(_BF)) + _dot(s.astype(_BF), v)
        qn = jnp.sum(q.astype(_F32) * n_old, axis=1, keepdims=True)
        den = w_inter * qn + jnp.sum(s, axis=1, keepdims=True)
        h = num / jnp.maximum(jnp.abs(den), jnp.exp(-mt))
        b_tot = b_c[0:1, :] if backward else b_c[T - 1:T, :]
        gl = b_tot - b_c + li_c
        m_new = jnp.maximum(b_tot + m, jnp.max(gl, axis=0, keepdims=True))
        decay = jnp.exp(b_tot + m - m_new)
        kw = k.astype(_F32) * (scale * jnp.exp(gl - m_new))
        c_ref[...] = decay * c_old + lax.dot_general(kw.astype(_BF), v, (((0,), (0,)), ((), ())),
                                                     preferred_element_type=_F32)
        n_ref[...] = decay * n_old + jnp.sum(kw, axis=0, keepdims=True)
        m_ref[...] = jnp.broadcast_to(m_new, m_ref.shape)
        return h

    def run(q_ref, k_ref, v_ref, o_ref, gc_ref, gr_ref, h_ref, z_ref):
        n_chunks = q_ref.shape[0] // T

        def fwd(ci, carry):
            r0 = pl.multiple_of(ci * T, T)
            h_ref[pl.ds(r0, T), :] = step(q_ref, k_ref, v_ref, gc_ref, gr_ref, r0, cf_ref, nf_ref, mf_ref, False)
            return carry
        lax.fori_loop(0, n_chunks, fwd, 0)

        def bwd(ci, carry):
            r0 = pl.multiple_of((n_chunks - 1 - ci) * T, T)
            hb = step(q_ref, k_ref, v_ref, gc_ref, gr_ref, r0, cb_ref, nb_ref, mb_ref, True)
            hs = h_ref[pl.ds(r0, T), :] + hb
            hn = hs * lax.rsqrt(jnp.mean(hs * hs, axis=-1, keepdims=True) + EPS)
            gate = _sigmoid(o_ref[pl.ds(r0, T), :].astype(_F32))
            z_ref[pl.ds(r0, T), :] = (hn * hg_ref[...] * gate).astype(_BF)
            return carry
        lax.fori_loop(0, n_chunks, bwd, 0)

    run(qc_ref, kc_ref, vc_ref, oc_ref, gcc_ref, grc_ref, hc_ref, zc_ref)
    run(qx_ref, kx_ref, vx_ref, ox_ref, gcx_ref, grx_ref, hx_ref, zx_ref)


def _scan(lay, proj, gcol, grow, hn_g, heads, chunk):
    d = lay.d
    dqk_all = (proj.shape[1] - 2 * d) // 2
    dqk, dv = dqk_all // heads, d // heads
    seq, ctx_len, batch = lay.seq, lay.ctx_len, lay.batch
    assert (2 * dqk_all) % dv == 0 and seq % ctx_len == 0 and seq % chunk == 0 and ctx_len % chunk == 0
    kq, vq, oq = heads, 2 * dqk_all // dv, (2 * dqk_all + d) // dv
    cb = lay.nx // ctx_len

    def xs_(width, off):
        return pl.BlockSpec((seq, width), lambda b, h: (b, off + h))

    def cs_(width, off):
        return pl.BlockSpec((ctx_len, width), lambda b, h: (cb + b, off + h))

    kern = functools.partial(_scan_kernel, chunk=chunk, scale=float(dqk) ** -0.5)
    return pl.pallas_call(
        kern,
        grid=(batch, heads),
        in_specs=[
            xs_(dqk, 0), xs_(dqk, kq), xs_(dv, vq), xs_(dv, oq),
            pl.BlockSpec((seq, LANES), lambda b, h: (b, h)),
            pl.BlockSpec((8, seq), lambda b, h: (h, b)),
            cs_(dqk, 0), cs_(dqk, kq), cs_(dv, vq), cs_(dv, oq),
            pl.BlockSpec((ctx_len, LANES), lambda b, h: (cb + b, h)),
            pl.BlockSpec((8, ctx_len), lambda b, h: (h, cb + b)),
            pl.BlockSpec((1, dv), lambda b, h: (0, h)),
        ],
        out_specs=[
            pl.BlockSpec((seq, dv), lambda b, h: (b, h)),
            pl.BlockSpec((ctx_len, dv), lambda b, h: (b, h)),
        ],
        out_shape=[
            jax.ShapeDtypeStruct((lay.nx, d), _BF),
            jax.ShapeDtypeStruct((lay.nc, d), _BF),
        ],
        scratch_shapes=[
            pltpu.VMEM((dqk, dv), _F32), pltpu.VMEM((1, dqk), _F32), pltpu.VMEM((8, LANES), _F32),
            pltpu.VMEM((dqk, dv), _F32), pltpu.VMEM((1, dqk), _F32), pltpu.VMEM((8, LANES), _F32),
            pltpu.VMEM((seq, dv), _F32), pltpu.VMEM((ctx_len, dv), _F32),
        ],
        compiler_params=_params("parallel", "parallel"),
        name="mlstm_scan",
    )(proj, proj, proj, proj, gcol, grow, proj, proj, proj, proj, gcol, grow, hn_g.reshape(1, d))


def _mout_res_kernel(zx_ref, zc_ref, w_ref, x_ref, gate_ref, o_ref, *, x_tiles):
    i = pl.program_id(0)

    @pl.when(i < x_tiles)
    def _():
        o_ref[...] = x_ref[...] + gate_ref[...] * _dot(zx_ref[...], w_ref[...])

    @pl.when(i >= x_tiles)
    def _():
        o_ref[...] = x_ref[...] + gate_ref[...] * _dot(zc_ref[...], w_ref[...])


def _mout_res(lay, zx, zc, xs, mods, layer, w_out, n_tiles):
    d = lay.d
    tm, tn = lay.tm, _tile(d, 512)
    xt = lay.x_tiles
    return pl.pallas_call(
        functools.partial(_mout_res_kernel, x_tiles=xt),
        grid=(n_tiles, d // tn),
        in_specs=[
            pl.BlockSpec((tm, d), lambda i, j: (jnp.minimum(i, xt - 1), 0)),
            pl.BlockSpec((tm, d), lambda i, j: (jnp.maximum(i - xt, 0), 0)),
            pl.BlockSpec((d, tn), lambda i, j: (0, j)),
            pl.BlockSpec((tm, tn), lambda i, j: (i, j)),
            _mod_spec(lay, layer, 2, tn),
        ],
        out_specs=pl.BlockSpec((tm, tn), lambda i, j: (i, j)),
        out_shape=jax.ShapeDtypeStruct((n_tiles * tm, d), _F32),
        compiler_params=_params("parallel", "parallel"),
        name="mlstm_out_res",
    )(zx, zc, w_out, xs, mods)


def kernel(x, c, ctx, c_ctx, norm1_g, norm2_g, w_mod, b_mod, w_gu, w_down, conv_w_pw1, conv_b_pw1,
           conv_w_dw, conv_b_dw, conv_ln_g, conv_ln_b, conv_w_pw2, conv_b_pw2, m_w_in, m_b_gates,
           m_hn_g, m_w_out, final_g):
    batch, seq, d = x.shape
    ctx_len = ctx.shape[1]
    depth = w_mod.shape[0]
    heads = max(4, d // 512)
    dqk_all = d // 2
    chunk = _tile(ctx_len, 256)
    assert batch + 1 <= MOD_ROWS and 4 * heads <= LANES
    lay = _Layout(batch, seq, ctx_len, d, 512)

    cc = jnp.zeros((MOD_ROWS, d), _F32).at[:batch].set(c).at[batch].set(c_ctx)
    mods = _modulation(cc, w_mod, b_mod).reshape(depth, MOD_ROWS, 6, 1, d)
    xs = jnp.concatenate([x.reshape(batch * seq, d), ctx.reshape(batch * ctx_len, d)], axis=0)

    w_gu_b, w_down_b = w_gu.astype(_BF), w_down.astype(_BF)
    w_pw1_b, w_pw2_b = conv_w_pw1.astype(_BF), conv_w_pw2.astype(_BF)
    n_main = 2 * dqk_all + 2 * d
    w_in_b = m_w_in[:, :, :n_main].astype(_BF)
    w_gates_b = jnp.zeros((m_w_in.shape[0], d, LANES), _BF).at[:, :, :4 * heads].set(
        m_w_in[:, :, n_main:].astype(_BF))
    w_out_b = m_w_out.astype(_BF)

    for i in range(depth):
        last = i == depth - 1
        j = i // 2
        n_tiles = lay.x_tiles if last else lay.all_tiles
        if i % 2 == 0:
            u = _pw1_glu(lay, xs, mods, i, norm1_g[i], w_pw1_b[j], conv_b_pw1[j])
            v = _dwconv(lay, u, conv_w_dw[j], conv_b_dw[j])
            xs = _pw2_res(lay, v, xs, mods, i, conv_ln_g[j], conv_ln_b[j], w_pw2_b[j], conv_b_pw2[j], n_tiles)
        else:
            proj, gates = _min_proj(lay, xs, mods, i, norm1_g[i], w_in_b[j], w_gates_b[j])
            gcol, grow = _gate_prep(lay, gates, m_b_gates[j], heads, chunk)
            zx, zc = _scan(lay, proj, gcol, grow, m_hn_g[j], heads, chunk)
            xs = _mout_res(lay, zx, zc, xs, mods, i, w_out_b[j], n_tiles)
        xs = _ffn(lay, xs, mods, i, norm2_g[i], w_gu_b[i], w_down_b[i], final_g, n_tiles, last)
    return xs[:batch * seq].reshape(batch, seq, d)
```

```python
import functools

import jax
import jax.numpy as jnp
from jax import lax
from jax.experimental import pallas as pl
from jax.experimental.pallas import tpu as pltpu

EPS = 1e-6
GATE_CAP = 15.0
GRID_W = 64
LANES = 128
SUBLANES = 8
VMEM_LIMIT = 56 * 1024 * 1024
MOD_ROWS = 32
ROW_TILE = 512
COL_TILE = 512

_BF = jnp.bfloat16
_F32 = jnp.float32


def _tile(dim, pref):
    t = min(dim, pref)
    while dim % t:
        t //= 2
    return t


def _params(*sem):
    return pltpu.CompilerParams(dimension_semantics=sem, vmem_limit_bytes=VMEM_LIMIT)


def _sigmoid(v):
    return 1.0 / (1.0 + jnp.exp(-v))


def _dot(a, b):
    return jnp.dot(a, b, preferred_element_type=_F32)


def _rms_mod(x, g, shift, scale):
    y = x * lax.rsqrt(jnp.mean(x * x, axis=-1, keepdims=True) + EPS)
    return (y * g) * (1.0 + scale) + shift


def _resident(shape):
    zeros = (0,) * len(shape)
    return pl.BlockSpec(shape, lambda *_: zeros, pipeline_mode=pl.Buffered(1))


def _mod_kernel(cc_ref, w_ref, b_ref, o_ref):
    cc = cc_ref[...]
    s = (cc * _sigmoid(cc)).astype(_BF)
    o_ref[...] = _dot(s, w_ref[...].astype(_BF)) + b_ref[...]


def _modulation(cc, w_mod, b_mod):
    depth, d, n = w_mod.shape
    tn = _tile(n, 1024)
    return pl.pallas_call(
        _mod_kernel,
        grid=(depth, n // tn),
        in_specs=[
            pl.BlockSpec((MOD_ROWS, d), lambda l, j: (0, 0)),
            pl.BlockSpec((None, d, tn), lambda l, j: (l, 0, j)),
            pl.BlockSpec((None, 1, tn), lambda l, j: (l, 0, j)),
        ],
        out_specs=pl.BlockSpec((None, MOD_ROWS, tn), lambda l, j: (l, 0, j)),
        out_shape=jax.ShapeDtypeStruct((depth, MOD_ROWS, n), _F32),
        compiler_params=_params("parallel", "parallel"),
        name="modulation",
    )(cc, w_mod, b_mod.reshape(depth, 1, n))


class _Stream:
    def __init__(self, n_rows, d, mod_row):
        self.n, self.d = n_rows, d
        self.tm = _tile(n_rows, ROW_TILE)
        self.tiles = n_rows // self.tm
        self.mod_row = mod_row

    def mod_spec(self, layer, part):
        return pl.BlockSpec((None, None, None, 1, self.d),
                            lambda i, *_: (layer, self.mod_row(i), part, 0, 0))


def _pw1_glu_kernel(x_ref, g_ref, sh_ref, sc_ref, w_ref, b_ref, u_ref, h_ref, *, tn):
    h_ref[...] = _rms_mod(x_ref[...], g_ref[...], sh_ref[...], sc_ref[...]).astype(_BF)
    d = h_ref.shape[1]
    for j in range(d // tn):
        lo, hi = j * tn, (j + 1) * tn
        a = _dot(h_ref[...], w_ref[:, lo:hi]) + b_ref[:, lo:hi]
        b = _dot(h_ref[...], w_ref[:, d + lo:d + hi]) + b_ref[:, d + lo:d + hi]
        u_ref[:, lo:hi] = (a * _sigmoid(b)).astype(u_ref.dtype)


def _pw1_glu(st, xs, mods, layer, g1, w_pw1, b_pw1):
    d, tm = st.d, st.tm
    return pl.pallas_call(
        functools.partial(_pw1_glu_kernel, tn=_tile(d, COL_TILE)),
        grid=(st.tiles,),
        in_specs=[
            pl.BlockSpec((tm, d), lambda i: (i, 0)),
            _resident((1, d)),
            st.mod_spec(layer, 0),
            st.mod_spec(layer, 1),
            _resident((d, 2 * d)),
            _resident((1, 2 * d)),
        ],
        out_specs=pl.BlockSpec((tm, d), lambda i: (i, 0)),
        out_shape=jax.ShapeDtypeStruct((st.n, d), _BF),
        scratch_shapes=[pltpu.VMEM((tm, d), _BF)],
        compiler_params=_params("parallel"),
        name="pw1_glu",
    )(xs, g1.reshape(1, d), mods, mods, w_pw1, b_pw1.reshape(1, 2 * d))


_CONV_LPAD = 16


def _conv_along_sublanes(pad_ref, w_ref, b_ref, n_rows, row_len, taps, store):
    tc = pad_ref.shape[-1]
    first = _CONV_LPAD - taps // 2

    def body(r, carry):
        for l0 in range(0, tc, LANES):
            lanes = pl.ds(l0, LANES)
            acc = jnp.zeros((row_len, LANES), _F32)
            for s in range(SUBLANES):
                ks = [k for k in range(taps) if (first + k) % SUBLANES == s]
                if not ks:
                    continue
                lo, hi = first + ks[0], first + ks[-1]
                span = pad_ref[r, pl.ds(lo, row_len + hi - lo), lanes]
                for k in ks:
                    off = first + k - lo
                    acc = acc + span[off:off + row_len] * w_ref[pl.ds(k, 1), lanes]
            store(r, l0, acc + b_ref[:, lanes])
        return carry
    lax.fori_loop(0, n_rows, body, 0)


def _dwconv_grid_kernel(u_ref, w_ref, b_ref, o_ref, padh_ref, padv_ref, *, n_h_tiles, taps):
    rows, _, tc = u_ref.shape
    half = taps // 2

    def store(r, l0, val):
        o_ref[r, :, pl.ds(l0, LANES)] = val.astype(o_ref.dtype)

    @pl.when(pl.program_id(1) < n_h_tiles)
    def _():
        zeros = jnp.zeros((rows, _CONV_LPAD, tc), _F32)
        padh_ref[:, pl.ds(0, _CONV_LPAD), :] = zeros
        padh_ref[:, pl.ds(_CONV_LPAD + GRID_W, _CONV_LPAD), :] = zeros
        padh_ref[:, pl.ds(_CONV_LPAD, GRID_W), :] = u_ref[...].astype(_F32)
        _conv_along_sublanes(padh_ref, w_ref, b_ref, rows, GRID_W, taps, store)

    @pl.when(pl.program_id(1) >= n_h_tiles)
    def _():
        zeros = jnp.zeros((half, GRID_W, tc), _F32)
        padv_ref[pl.ds(0, half)] = zeros
        padv_ref[pl.ds(half + rows, half)] = zeros
        padv_ref[pl.ds(half, rows)] = u_ref[...].astype(_F32)

        def body(r, carry):
            for l0 in range(0, tc, LANES):
                lanes = pl.ds(l0, LANES)
                acc = jnp.zeros((GRID_W, LANES), _F32)
                for k in range(taps):
                    acc = acc + padv_ref[r + k, :, lanes] * w_ref[pl.ds(k, 1), lanes]
                store(r, l0, acc + b_ref[:, lanes])
            return carry
        lax.fori_loop(0, rows, body, 0)


def _dwconv_seq_kernel(u_ref, w_ref, b_ref, o_ref, pad_ref, *, taps):
    n_seq, seq_len, tc = u_ref.shape
    zeros = jnp.zeros((n_seq, _CONV_LPAD, tc), _F32)
    pad_ref[:, pl.ds(0, _CONV_LPAD), :] = zeros
    pad_ref[:, pl.ds(_CONV_LPAD + seq_len, _CONV_LPAD), :] = zeros
    pad_ref[:, pl.ds(_CONV_LPAD, seq_len), :] = u_ref[...].astype(_F32)

    def store(r, l0, val):
        o_ref[r, :, pl.ds(l0, LANES)] = val.astype(o_ref.dtype)
    _conv_along_sublanes(pad_ref, w_ref, b_ref, n_seq, seq_len, taps, store)


def _dwconv(u, batch, seq, on_grid, w_dw, b_dw):
    n, d = u.shape
    taps = w_dw.shape[0]
    assert taps // 2 <= _CONV_LPAD
    tc = _tile(d // 2, 256)
    wspecs = [pl.BlockSpec((taps, tc), lambda i, c: (0, c)), pl.BlockSpec((1, tc), lambda i, c: (0, c))]
    if on_grid:
        assert seq % GRID_W == 0
        rows = seq // GRID_W
        block = pl.BlockSpec((rows, GRID_W, tc), lambda i, c: (i, 0, c))
        out = pl.pallas_call(
            functools.partial(_dwconv_grid_kernel, n_h_tiles=(d // 2) // tc, taps=taps),
            grid=(batch, d // tc),
            in_specs=[block] + wspecs,
            out_specs=block,
            out_shape=jax.ShapeDtypeStruct((n // GRID_W, GRID_W, d), u.dtype),
            scratch_shapes=[
                pltpu.VMEM((rows, GRID_W + 2 * _CONV_LPAD, tc), _F32),
                pltpu.VMEM((rows + 2 * (taps // 2), GRID_W, tc), _F32),
            ],
            compiler_params=_params("parallel", "parallel"),
            name="dwconv_grid",
        )(u.reshape(n // GRID_W, GRID_W, d), w_dw, b_dw.reshape(1, d))
    else:
        bb = _tile(batch, 8)
        block = pl.BlockSpec((bb, seq, tc), lambda i, c: (i, 0, c))
        out = pl.pallas_call(
            functools.partial(_dwconv_seq_kernel, taps=taps),
            grid=(batch // bb, d // tc),
            in_specs=[block] + wspecs,
            out_specs=block,
            out_shape=jax.ShapeDtypeStruct((batch, seq, d), u.dtype),
            scratch_shapes=[pltpu.VMEM((bb, seq + 2 * _CONV_LPAD, tc), _F32)],
            compiler_params=_params("parallel", "parallel"),
            name="dwconv_seq",
        )(u.reshape(batch, seq, d), w_dw, b_dw.reshape(1, d))
    return out.reshape(n, d)


def _ffn_step(h_ref, gate2_ref, wg_ref, wu_ref, wd_ref, fg_ref, o_ref, final_norm):
    h = h_ref[...]
    g = _dot(h, wg_ref[...])
    u = _dot(h, wu_ref[...])
    act = (g * _sigmoid(g) * u).astype(_BF)
    o_ref[...] += gate2_ref[...] * _dot(act, wd_ref[...])
    if final_norm:
        @pl.when(pl.program_id(1) == pl.num_programs(1) - 1)
        def _():
            y = o_ref[...]
            o_ref[...] = y * lax.rsqrt(jnp.mean(y * y, axis=-1, keepdims=True) + EPS) * fg_ref[...]


def _conv_out_ffn_kernel(v_ref, lg_ref, lb_ref, w2_ref, b2_ref, x_ref, gate1_ref, g_ref, sh_ref, sc_ref,
                         gate2_ref, wg_ref, wu_ref, wd_ref, fg_ref, o_ref, h_ref, *, final_norm):
    @pl.when(pl.program_id(1) == 0)
    def _():
        v = v_ref[...].astype(_F32)
        mu = jnp.mean(v, axis=-1, keepdims=True)
        vc = v - mu
        var = jnp.mean(vc * vc, axis=-1, keepdims=True)
        y = vc * lax.rsqrt(var + EPS) * lg_ref[...] + lb_ref[...]
        z = (y * _sigmoid(y)).astype(_BF)
        x1 = x_ref[...] + gate1_ref[...] * (_dot(z, w2_ref[...]) + b2_ref[...])
        o_ref[...] = x1
        h_ref[...] = _rms_mod(x1, g_ref[...], sh_ref[...], sc_ref[...]).astype(_BF)

    _ffn_step(h_ref, gate2_ref, wg_ref, wu_ref, wd_ref, fg_ref, o_ref, final_norm)


def _mlstm_out_ffn_kernel(z_ref, w2_ref, x_ref, gate1_ref, g_ref, sh_ref, sc_ref,
                          gate2_ref, wg_ref, wu_ref, wd_ref, fg_ref, o_ref, h_ref, *, final_norm):
    @pl.when(pl.program_id(1) == 0)
    def _():
        x1 = x_ref[...] + gate1_ref[...] * _dot(z_ref[...], w2_ref[...])
        o_ref[...] = x1
        h_ref[...] = _rms_mod(x1, g_ref[...], sh_ref[...], sc_ref[...]).astype(_BF)

    _ffn_step(h_ref, gate2_ref, wg_ref, wu_ref, wd_ref, fg_ref, o_ref, final_norm)


def _mix_out_ffn(st, mix_args, mix_specs, kern, xs, mods, layer, g2, w_gu, w_down, final_g, final_norm):
    d, tm = st.d, st.tm
    ff = w_down.shape[0]
    tf = _tile(ff, COL_TILE)
    nf = ff // tf
    row_block = pl.BlockSpec((tm, d), lambda i, f: (i, 0))
    return pl.pallas_call(
        functools.partial(kern, final_norm=final_norm),
        grid=(st.tiles, nf),
        in_specs=mix_specs + [
            row_block,
            st.mod_spec(layer, 2),
            _resident((1, d)),
            st.mod_spec(layer, 3),
            st.mod_spec(layer, 4),
            st.mod_spec(layer, 5),
            pl.BlockSpec((d, tf), lambda i, f: (0, f)),
            pl.BlockSpec((d, tf), lambda i, f: (0, nf + f)),
            pl.BlockSpec((tf, d), lambda i, f: (f, 0)),
            _resident((1, d)),
        ],
        out_specs=row_block,
        out_shape=jax.ShapeDtypeStruct((st.n, d), _F32),
        scratch_shapes=[pltpu.VMEM((tm, d), _BF)],
        compiler_params=_params("parallel", "arbitrary"),
        name=kern.__name__.strip("_").replace("_kernel", ""),
    )(*mix_args, xs, mods, g2.reshape(1, d), mods, mods, mods, w_gu, w_gu, w_down, final_g.reshape(1, d))


def _conv_out_ffn(st, v, xs, mods, layer, ln_g, ln_b, w_pw2, b_pw2, g2, w_gu, w_down, final_g, final_norm):
    d = st.d
    specs = [pl.BlockSpec((st.tm, d), lambda i, f: (i, 0)), _resident((1, d)), _resident((1, d)),
             _resident((d, d)), _resident((1, d))]
    args = (v, ln_g.reshape(1, d), ln_b.reshape(1, d), w_pw2, b_pw2.reshape(1, d))
    return _mix_out_ffn(st, args, specs, _conv_out_ffn_kernel, xs, mods, layer, g2, w_gu, w_down,
                        final_g, final_norm)


def _mlstm_out_ffn(st, z, xs, mods, layer, w_out, g2, w_gu, w_down, final_g, final_norm):
    d = st.d
    specs = [pl.BlockSpec((st.tm, d), lambda i, f: (i, 0)), _resident((d, d))]
    return _mix_out_ffn(st, (z, w_out), specs, _mlstm_out_ffn_kernel, xs, mods, layer, g2, w_gu, w_down,
                        final_g, final_norm)


def _min_proj_kernel(x_ref, g_ref, sh_ref, sc_ref, w_ref, wg_ref, p_ref, gt_ref, h_ref, *, tn):
    h_ref[...] = _rms_mod(x_ref[...], g_ref[...], sh_ref[...], sc_ref[...]).astype(_BF)
    gt_ref[...] = _dot(h_ref[...], wg_ref[...])
    for j in range(p_ref.shape[1] // tn):
        lo, hi = j * tn, (j + 1) * tn
        p_ref[:, lo:hi] = _dot(h_ref[...], w_ref[:, lo:hi]).astype(_BF)


def _min_proj(st, xs, mods, layer, g1, w_main, w_gates):
    d, tm = st.d, st.tm
    n_main = w_main.shape[1]
    return pl.pallas_call(
        functools.partial(_min_proj_kernel, tn=_tile(n_main, COL_TILE)),
        grid=(st.tiles,),
        in_specs=[
            pl.BlockSpec((tm, d), lambda i: (i, 0)),
            _resident((1, d)),
            st.mod_spec(layer, 0),
            st.mod_spec(layer, 1),
            _resident((d, n_main)),
            _resident((d, LANES)),
        ],
        out_specs=[
            pl.BlockSpec((tm, n_main), lambda i: (i, 0)),
            pl.BlockSpec((tm, LANES), lambda i: (i, 0)),
        ],
        out_shape=[
            jax.ShapeDtypeStruct((st.n, n_main), _BF),
            jax.ShapeDtypeStruct((st.n, LANES), _F32),
        ],
        scratch_shapes=[pltpu.VMEM((tm, d), _BF)],
        compiler_params=_params("parallel"),
        name="mlstm_in_proj",
    )(xs, g1.reshape(1, d), mods, mods, w_main, w_gates)


def _log_sigmoid(v):
    return jnp.minimum(v, 0.0) - jnp.log(1.0 + jnp.exp(-jnp.abs(v)))


def _gate_prep_kernel(g_ref, b_ref, col_ref, row_ref, *, heads, chunk):
    rows = g_ref.shape[0]
    g = g_ref[...] + b_ref[...]
    g = GATE_CAP * jnp.tanh(g / GATE_CAP)
    lane = lax.broadcasted_iota(jnp.int32, g.shape, 1)
    is_f = (lane // heads) % 2 == 1
    vals = jnp.where(is_f, _log_sigmoid(g), g)
    t = lax.broadcasted_iota(jnp.int32, g.shape, 0) % chunk
    pre = vals
    suf = vals
    shift = 1
    while shift < chunk:
        pre = pre + jnp.where(t >= shift, pltpu.roll(pre, shift, 0), 0.0)
        suf = suf + jnp.where(t + shift < chunk, pltpu.roll(suf, rows - shift, 0), 0.0)
        shift *= 2
    lane_o = lax.broadcasted_iota(jnp.int32, (rows, LANES), 1)
    for h in range(heads):
        li_f = vals[:, h:h + 1]
        b_f = pre[:, heads + h:heads + h + 1]
        li_b = vals[:, 2 * heads + h:2 * heads + h + 1]
        b_b = suf[:, 3 * heads + h:3 * heads + h + 1]
        blk = jnp.where(lane_o == 0, li_f, jnp.where(lane_o == 1, b_f, jnp.where(lane_o == 2, li_b, b_b)))
        col_ref[:, pl.ds(h * LANES, LANES)] = blk
        row_ref[pl.ds(h * SUBLANES, SUBLANES), :] = blk.T[0:SUBLANES, :]


def _gate_prep(st, gates, b_gates, heads, chunk):
    n, tr = st.n, st.tm
    assert tr % chunk == 0
    bias = jnp.zeros((1, LANES), _F32).at[0, :4 * heads].set(b_gates)
    return pl.pallas_call(
        functools.partial(_gate_prep_kernel, heads=heads, chunk=chunk),
        grid=(n // tr,),
        in_specs=[
            pl.BlockSpec((tr, LANES), lambda i: (i, 0)),
            pl.BlockSpec((1, LANES), lambda i: (0, 0)),
        ],
        out_specs=[
            pl.BlockSpec((tr, heads * LANES), lambda i: (i, 0)),
            pl.BlockSpec((heads * SUBLANES, tr), lambda i: (0, i)),
        ],
        out_shape=[
            jax.ShapeDtypeStruct((n, heads * LANES), _F32),
            jax.ShapeDtypeStruct((heads * SUBLANES, n), _F32),
        ],
        compiler_params=_params("parallel"),
        name="mlstm_gate_prep",
    )(gates, bias)


def _scan_kernel(qx_ref, kx_ref, vx_ref, ox_ref, gcx_ref, grx_ref,
                 qc_ref, kc_ref, vc_ref, oc_ref, gcc_ref, grc_ref, hg_ref,
                 zx_ref, zc_ref,
                 cf_ref, nf_ref, mf_ref, cb_ref, nb_ref, mb_ref, hx_ref, hc_ref, *, chunk, scale):
    T = chunk
    row_i = lax.broadcasted_iota(jnp.int32, (T, T), 0)
    col_i = lax.broadcasted_iota(jnp.int32, (T, T), 1)

    for ref in (cf_ref, nf_ref, mf_ref, cb_ref, nb_ref, mb_ref):
        ref[...] = jnp.zeros_like(ref)

    def step(q_ref, k_ref, v_ref, gc_ref, gr_ref, r0, c_ref, n_ref, m_ref, backward):
        rows = pl.ds(r0, T)
        q = q_ref[rows, :]
        k = k_ref[rows, :]
        v = v_ref[rows, :]
        gcb = gc_ref[rows, :]
        grb = gr_ref[:, rows]
        o = 2 if backward else 0
        li_c, b_c = gcb[:, o:o + 1], gcb[:, o + 1:o + 2]
        li_r, b_r = grb[o:o + 1, :], grb[o + 1:o + 2, :]
        m = m_ref[0:1, 0:1]
        a = b_c + m
        dmat = b_c - b_r + li_r
        dmat = jnp.where((row_i <= col_i) if backward else (row_i >= col_i), dmat, -jnp.inf)
        mt = jnp.maximum(a, jnp.max(dmat, axis=1, keepdims=True))
        w_inter = jnp.exp(a - mt)
        qk = lax.dot_general(q, k, (((1,), (1,)), ((), ())), preferred_element_type=_F32)
        s = qk * scale * jnp.exp(dmat - mt)
        c_old = c_ref[...]
        n_old = n_ref[...]
        num = w_inter * _dot(q, c_old.astype(_BF)) + _dot(s.astype(_BF), v)
        qn = jnp.sum(q.astype(_F32) * n_old, axis=1, keepdims=True)
        den = w_inter * qn + jnp.sum(s, axis=1, keepdims=True)
        h = num / jnp.maximum(jnp.abs(den), jnp.exp(-mt))
        b_tot = b_c[0:1, :] if backward else b_c[T - 1:T, :]
        gl = b_tot - b_c + li_c
        m_new = jnp.maximum(b_tot + m, jnp.max(gl, axis=0, keepdims=True))
        decay = jnp.exp(b_tot + m - m_new)
        kw = k.astype(_F32) * (scale * jnp.exp(gl - m_new))
        c_ref[...] = decay * c_old + lax.dot_general(kw.astype(_BF), v, (((0,), (0,)), ((), ())),
                                                     preferred_element_type=_F32)
        n_ref[...] = decay * n_old + jnp.sum(kw, axis=0, keepdims=True)
        m_ref[...] = jnp.broadcast_to(m_new, m_ref.shape)
        return h

    def run(q_ref, k_ref, v_ref, o_ref, gc_ref, gr_ref, h_ref, z_ref):
        n_chunks = q_ref.shape[0] // T

        def fwd(ci, carry):
            r0 = pl.multiple_of(ci * T, T)
            h_ref[pl.ds(r0, T), :] = step(q_ref, k_ref, v_ref, gc_ref, gr_ref, r0, cf_ref, nf_ref, mf_ref, False)
            return carry
        lax.fori_loop(0, n_chunks, fwd, 0)

        def bwd(ci, carry):
            r0 = pl.multiple_of((n_chunks - 1 - ci) * T, T)
            hb = step(q_ref, k_ref, v_ref, gc_ref, gr_ref, r0, cb_ref, nb_ref, mb_ref, True)
            hs = h_ref[pl.ds(r0, T), :] + hb
            hn = hs * lax.rsqrt(jnp.mean(hs * hs, axis=-1, keepdims=True) + EPS)
            gate = _sigmoid(o_ref[pl.ds(r0, T), :].astype(_F32))
            z_ref[pl.ds(r0, T), :] = (hn * hg_ref[...] * gate).astype(_BF)
            return carry
        lax.fori_loop(0, n_chunks, bwd, 0)

    run(qc_ref, kc_ref, vc_ref, oc_ref, gcc_ref, grc_ref, hc_ref, zc_ref)
    run(qx_ref, kx_ref, vx_ref, ox_ref, gcx_ref, grx_ref, hx_ref, zx_ref)


def _scan(batch, seq, ctx_len, d, proj_x, gcol_x, grow_x, proj_c, gcol_c, grow_c, hn_g, heads, chunk):
    dqk_all = (proj_x.shape[1] - 2 * d) // 2
    dqk, dv = dqk_all // heads, d // heads
    assert (2 * dqk_all) % dv == 0 and seq % chunk == 0 and ctx_len % chunk == 0
    kq, vq, oq = heads, 2 * dqk_all // dv, (2 * dqk_all + d) // dv

    def stream_specs(length):
        def blk(width, off):
            return pl.BlockSpec((length, width), lambda b, h: (b, off + h))
        return [blk(dqk, 0), blk(dqk, kq), blk(dv, vq), blk(dv, oq),
                pl.BlockSpec((length, LANES), lambda b, h: (b, h)),
                pl.BlockSpec((SUBLANES, length), lambda b, h: (h, b))]

    kern = functools.partial(_scan_kernel, chunk=chunk, scale=float(dqk) ** -0.5)
    return pl.pallas_call(
        kern,
        grid=(batch, heads),
        in_specs=stream_specs(seq) + stream_specs(ctx_len) + [pl.BlockSpec((1, dv), lambda b, h: (0, h))],
        out_specs=[
            pl.BlockSpec((seq, dv), lambda b, h: (b, h)),
            pl.BlockSpec((ctx_len, dv), lambda b, h: (b, h)),
        ],
        out_shape=[
            jax.ShapeDtypeStruct((batch * seq, d), _BF),
            jax.ShapeDtypeStruct((batch * ctx_len, d), _BF),
        ],
        scratch_shapes=[
            pltpu.VMEM((dqk, dv), _F32), pltpu.VMEM((1, dqk), _F32), pltpu.VMEM((SUBLANES, LANES), _F32),
            pltpu.VMEM((dqk, dv), _F32), pltpu.VMEM((1, dqk), _F32), pltpu.VMEM((SUBLANES, LANES), _F32),
            pltpu.VMEM((seq, dv), _F32), pltpu.VMEM((ctx_len, dv), _F32),
        ],
        compiler_params=_params("parallel", "parallel"),
        name="mlstm_scan",
    )(proj_x, proj_x, proj_x, proj_x, gcol_x, grow_x,
      proj_c, proj_c, proj_c, proj_c, gcol_c, grow_c, hn_g.reshape(1, d))


def kernel(x, c, ctx, c_ctx, norm1_g, norm2_g, w_mod, b_mod, w_gu, w_down, conv_w_pw1, conv_b_pw1,
           conv_w_dw, conv_b_dw, conv_ln_g, conv_ln_b, conv_w_pw2, conv_b_pw2, m_w_in, m_b_gates,
           m_hn_g, m_w_out, final_g):
    batch, seq, d = x.shape
    ctx_len = ctx.shape[1]
    depth = w_mod.shape[0]
    heads = max(4, d // 512)
    dqk_all = d // 2
    chunk = _tile(ctx_len, 256)
    assert batch + 1 <= MOD_ROWS and 4 * heads <= LANES

    sx = _Stream(batch * seq, d, None)
    assert seq % sx.tm == 0
    tiles_per_sample = seq // sx.tm
    sx.mod_row = lambda i: i // tiles_per_sample
    sc = _Stream(batch * ctx_len, d, lambda i: batch)

    cc = jnp.zeros((MOD_ROWS, d), _F32).at[:batch].set(c).at[batch].set(c_ctx)
    mods = _modulation(cc, w_mod, b_mod).reshape(depth, MOD_ROWS, 6, 1, d)

    w_gu_b, w_down_b = w_gu.astype(_BF), w_down.astype(_BF)
    w_pw1_b, w_pw2_b = conv_w_pw1.astype(_BF), conv_w_pw2.astype(_BF)
    n_main = 2 * dqk_all + 2 * d
    w_in_b = m_w_in[:, :, :n_main].astype(_BF)
    w_gates_b = jnp.zeros((m_w_in.shape[0], d, LANES), _BF).at[:, :, :4 * heads].set(
        m_w_in[:, :, n_main:].astype(_BF))
    w_out_b = m_w_out.astype(_BF)

    xs = x.reshape(batch * seq, d)
    cs = ctx.reshape(batch * ctx_len, d)
    for i in range(depth):
        last = i == depth - 1
        j = i // 2
        ffn_w = (norm2_g[i], w_gu_b[i], w_down_b[i], final_g)
        if i % 2 == 0:
            conv_w = (conv_ln_g[j], conv_ln_b[j], w_pw2_b[j], conv_b_pw2[j])
            ux = _pw1_glu(sx, xs, mods, i, norm1_g[i], w_pw1_b[j], conv_b_pw1[j])
            vx = _dwconv(ux, batch, seq, True, conv_w_dw[j], conv_b_dw[j])
            if not last:
                uc = _pw1_glu(sc, cs, mods, i, norm1_g[i], w_pw1_b[j], conv_b_pw1[j])
                vc = _dwconv(uc, batch, ctx_len, False, conv_w_dw[j], conv_b_dw[j])
                cs = _conv_out_ffn(sc, vc, cs, mods, i, *conv_w, *ffn_w, False)
            xs = _conv_out_ffn(sx, vx, xs, mods, i, *conv_w, *ffn_w, last)
        else:
            px, gx = _min_proj(sx, xs, mods, i, norm1_g[i], w_in_b[j], w_gates_b[j])
            pc, gc = _min_proj(sc, cs, mods, i, norm1_g[i], w_in_b[j], w_gates_b[j])
            gcol_x, grow_x = _gate_prep(sx, gx, m_b_gates[j], heads, chunk)
            gcol_c, grow_c = _gate_prep(sc, gc, m_b_gates[j], heads, chunk)
            zx, zc = _scan(batch, seq, ctx_len, d, px, gcol_x, grow_x, pc, gcol_c, grow_c,
                           m_hn_g[j], heads, chunk)
            if not last:
                cs = _mlstm_out_ffn(sc, zc, cs, mods, i, w_out_b[j], *ffn_w, False)
            xs = _mlstm_out_ffn(sx, zx, xs, mods, i, w_out_b[j], *ffn_w, last)
    return xs.reshape(batch, seq, d)
```

```python
import functools

import jax
import jax.numpy as jnp
from jax import lax
from jax.experimental import pallas as pl
from jax.experimental.pallas import tpu as pltpu

EPS = 1e-6
GATE_CAP = 15.0
GRID_W = 64
LANES = 128
SUBLANES = 8
VMEM_LIMIT = 56 * 1024 * 1024
MOD_ROWS = 32
ROW_TILE = 512
COL_TILE = 512

_BF = jnp.bfloat16
_F32 = jnp.float32


def _tile(dim, pref):
    t = min(dim, pref)
    while dim % t:
        t //= 2
    return t


def _params(*sem):
    return pltpu.CompilerParams(dimension_semantics=sem, vmem_limit_bytes=VMEM_LIMIT)


def _sigmoid(v):
    return 0.5 * (1.0 + jnp.tanh(0.5 * v))


def _silu(v):
    h = 0.5 * v
    return h + h * jnp.tanh(h)


def _lane_tile(rep, width):
    if width % LANES == 0:
        return rep if width == LANES else jnp.concatenate([rep] * (width // LANES), axis=1)
    assert width < LANES
    return rep[:, :width]


def _dot(a, b):
    return jnp.dot(a, b, preferred_element_type=_F32)


def _rms_mod(x, g, shift, scale):
    y = x * lax.rsqrt(jnp.mean(x * x, axis=-1, keepdims=True) + EPS)
    return (y * g) * (1.0 + scale) + shift


def _resident(shape):
    zeros = (0,) * len(shape)
    return pl.BlockSpec(shape, lambda *_: zeros, pipeline_mode=pl.Buffered(1))


class _Layered:
    def __init__(self, stacked, index):
        self.arr, self.index = stacked, index
        self.shape = stacked.shape[1:]

    def resident(self, shape=None):
        shape = self.shape if shape is None else shape
        index = (self.index,) + (0,) * len(shape)
        return pl.BlockSpec((None,) + tuple(shape), lambda *_: index, pipeline_mode=pl.Buffered(1))

    def tiled(self, block, index_map):
        return pl.BlockSpec((None,) + tuple(block), lambda *a: (self.index,) + tuple(index_map(*a)))


def _mod_kernel(cc_ref, w_ref, b_ref, o_ref):
    cc = cc_ref[...]
    s = _silu(cc).astype(_BF)
    o_ref[...] = _dot(s, w_ref[...].astype(_BF)) + b_ref[...]


def _modulation(cc, w_mod, b_mod):
    depth, d, n = w_mod.shape
    tn = _tile(n, 1024)
    return pl.pallas_call(
        _mod_kernel,
        grid=(depth, n // tn),
        in_specs=[
            pl.BlockSpec((MOD_ROWS, d), lambda l, j: (0, 0)),
            pl.BlockSpec((None, d, tn), lambda l, j: (l, 0, j)),
            pl.BlockSpec((None, 1, tn), lambda l, j: (l, 0, j)),
        ],
        out_specs=pl.BlockSpec((None, MOD_ROWS, tn), lambda l, j: (l, 0, j)),
        out_shape=jax.ShapeDtypeStruct((depth, MOD_ROWS, n), _F32),
        compiler_params=_params("parallel", "parallel"),
        name="modulation",
    )(cc, w_mod, b_mod.reshape(depth, 1, n))


class _Stream:
    def __init__(self, n_rows, d, mod_row):
        self.n, self.d = n_rows, d
        self.tm = _tile(n_rows, ROW_TILE)
        self.tiles = n_rows // self.tm
        self.mod_row = mod_row

    def mod_spec(self, layer, part):
        return pl.BlockSpec((None, None, None, 1, self.d),
                            lambda i, *_: (layer, self.mod_row(i), part, 0, 0))


def _pw1_glu_kernel(x_ref, g_ref, sh_ref, sc_ref, w_ref, b_ref, u_ref, h_ref, *, tn):
    h_ref[...] = _rms_mod(x_ref[...], g_ref[...], sh_ref[...], sc_ref[...]).astype(_BF)
    d = h_ref.shape[1]
    for j in range(d // tn):
        lo, hi = j * tn, (j + 1) * tn
        a = _dot(h_ref[...], w_ref[:, lo:hi]) + b_ref[:, lo:hi]
        b = _dot(h_ref[...], w_ref[:, d + lo:d + hi]) + b_ref[:, d + lo:d + hi]
        u_ref[:, lo:hi] = (a * _sigmoid(b)).astype(u_ref.dtype)


def _pw1_glu(st, xs, mods, layer, g1, w_pw1, b_pw1):
    d, tm = st.d, st.tm
    return pl.pallas_call(
        functools.partial(_pw1_glu_kernel, tn=_tile(d, COL_TILE)),
        grid=(st.tiles,),
        in_specs=[
            pl.BlockSpec((tm, d), lambda i: (i, 0)),
            _resident((1, d)),
            st.mod_spec(layer, 0),
            st.mod_spec(layer, 1),
            w_pw1.resident(),
            _resident((1, 2 * d)),
        ],
        out_specs=pl.BlockSpec((tm, d), lambda i: (i, 0)),
        out_shape=jax.ShapeDtypeStruct((st.n, d), _BF),
        scratch_shapes=[pltpu.VMEM((tm, d), _BF)],
        compiler_params=_params("parallel"),
        name="pw1_glu",
    )(xs, g1.reshape(1, d), mods, mods, w_pw1.arr, b_pw1.reshape(1, 2 * d))


_CONV_LPAD = 16


def _conv_along_sublanes(pad_ref, w_ref, b_ref, n_rows, row_len, taps, store):
    tc = pad_ref.shape[-1]
    first = _CONV_LPAD - taps // 2

    def body(r, carry):
        for l0 in range(0, tc, LANES):
            lanes = pl.ds(l0, LANES)
            acc = jnp.zeros((row_len, LANES), _F32)
            for s in range(SUBLANES):
                ks = [k for k in range(taps) if (first + k) % SUBLANES == s]
                if not ks:
                    continue
                lo, hi = first + ks[0], first + ks[-1]
                n = row_len + hi - lo + (SUBLANES if s else 0)
                span = pad_ref[r, pl.ds(lo - s, n), lanes]
                if s:
                    span = pltpu.roll(span, n - s, 0)
                for k in ks:
                    off = first + k - lo
                    acc = acc + span[off:off + row_len] * w_ref[pl.ds(k, 1), lanes]
            store(r, l0, acc + b_ref[:, lanes])
        return carry
    lax.fori_loop(0, n_rows, body, 0)


def _dwconv_grid_kernel(u_ref, w_ref, b_ref, o_ref, padh_ref, padv_ref, *, n_h_tiles, taps):
    rows, _, tc = u_ref.shape
    half = taps // 2

    def store(r, l0, val):
        o_ref[r, :, pl.ds(l0, LANES)] = val.astype(o_ref.dtype)

    @pl.when(pl.program_id(1) < n_h_tiles)
    def _():
        zeros = jnp.zeros((rows, _CONV_LPAD, tc), _F32)
        padh_ref[:, pl.ds(0, _CONV_LPAD), :] = zeros
        padh_ref[:, pl.ds(_CONV_LPAD + GRID_W, _CONV_LPAD), :] = zeros
        padh_ref[:, pl.ds(_CONV_LPAD, GRID_W), :] = u_ref[...].astype(_F32)
        _conv_along_sublanes(padh_ref, w_ref, b_ref, rows, GRID_W, taps, store)

    @pl.when(pl.program_id(1) >= n_h_tiles)
    def _():
        zeros = jnp.zeros((half, GRID_W, tc), _F32)
        padv_ref[pl.ds(0, half)] = zeros
        padv_ref[pl.ds(half + rows, half)] = zeros
        padv_ref[pl.ds(half, rows)] = u_ref[...].astype(_F32)

        def body(r, carry):
            for l0 in range(0, tc, LANES):
                lanes = pl.ds(l0, LANES)
                acc = jnp.zeros((GRID_W, LANES), _F32)
                for k in range(taps):
                    acc = acc + padv_ref[r + k, :, lanes] * w_ref[pl.ds(k, 1), lanes]
                store(r, l0, acc + b_ref[:, lanes])
            return carry
        lax.fori_loop(0, rows, body, 0)


def _dwconv_seq_kernel(u_ref, w_ref, b_ref, o_ref, pad_ref, *, taps):
    n_seq, seq_len, tc = u_ref.shape
    zeros = jnp.zeros((n_seq, _CONV_LPAD, tc), _F32)
    pad_ref[:, pl.ds(0, _CONV_LPAD), :] = zeros
    pad_ref[:, pl.ds(_CONV_LPAD + seq_len, _CONV_LPAD), :] = zeros
    pad_ref[:, pl.ds(_CONV_LPAD, seq_len), :] = u_ref[...].astype(_F32)

    def store(r, l0, val):
        o_ref[r, :, pl.ds(l0, LANES)] = val.astype(o_ref.dtype)
    _conv_along_sublanes(pad_ref, w_ref, b_ref, n_seq, seq_len, taps, store)


def _dwconv(u, batch, seq, on_grid, w_dw, b_dw):
    n, d = u.shape
    taps = w_dw.shape[0]
    assert taps // 2 <= _CONV_LPAD
    tc = _tile(d // 2, 256)
    wspecs = [pl.BlockSpec((taps, tc), lambda i, c: (0, c)), pl.BlockSpec((1, tc), lambda i, c: (0, c))]
    if on_grid:
        assert seq % GRID_W == 0
        rows = seq // GRID_W
        block = pl.BlockSpec((rows, GRID_W, tc), lambda i, c: (i, 0, c))
        out = pl.pallas_call(
            functools.partial(_dwconv_grid_kernel, n_h_tiles=(d // 2) // tc, taps=taps),
            grid=(batch, d // tc),
            in_specs=[block] + wspecs,
            out_specs=block,
            out_shape=jax.ShapeDtypeStruct((n // GRID_W, GRID_W, d), u.dtype),
            scratch_shapes=[
                pltpu.VMEM((rows, GRID_W + 2 * _CONV_LPAD, tc), _F32),
                pltpu.VMEM((rows + 2 * (taps // 2), GRID_W, tc), _F32),
            ],
            compiler_params=_params("parallel", "parallel"),
            name="dwconv_grid",
        )(u.reshape(n // GRID_W, GRID_W, d), w_dw, b_dw.reshape(1, d))
    else:
        bb = _tile(batch, 8)
        block = pl.BlockSpec((bb, seq, tc), lambda i, c: (i, 0, c))
        out = pl.pallas_call(
            functools.partial(_dwconv_seq_kernel, taps=taps),
            grid=(batch // bb, d // tc),
            in_specs=[block] + wspecs,
            out_specs=block,
            out_shape=jax.ShapeDtypeStruct((batch, seq, d), u.dtype),
            scratch_shapes=[pltpu.VMEM((bb, seq + 2 * _CONV_LPAD, tc), _F32)],
            compiler_params=_params("parallel", "parallel"),
            name="dwconv_seq",
        )(u.reshape(batch, seq, d), w_dw, b_dw.reshape(1, d))
    return out.reshape(n, d)


def _ffn_step(h_ref, gate2_ref, wg_ref, wu_ref, wd_ref, fg_ref, o_ref, final_norm):
    h = h_ref[...]
    g = _dot(h, wg_ref[...])
    u = _dot(h, wu_ref[...])
    act = (_silu(g) * u).astype(_BF)
    o_ref[...] += gate2_ref[...] * _dot(act, wd_ref[...])
    if final_norm:
        @pl.when(pl.program_id(1) == pl.num_programs(1) - 1)
        def _():
            y = o_ref[...]
            o_ref[...] = y * lax.rsqrt(jnp.mean(y * y, axis=-1, keepdims=True) + EPS) * fg_ref[...]


def _conv_out_ffn_kernel(v_ref, lg_ref, lb_ref, w2_ref, b2_ref, x_ref, gate1_ref, g_ref, sh_ref, sc_ref,
                         gate2_ref, wg_ref, wu_ref, wd_ref, fg_ref, o_ref, h_ref, *, final_norm):
    @pl.when(pl.program_id(1) == 0)
    def _():
        v = v_ref[...].astype(_F32)
        mu = jnp.mean(v, axis=-1, keepdims=True)
        vc = v - mu
        var = jnp.mean(vc * vc, axis=-1, keepdims=True)
        y = vc * lax.rsqrt(var + EPS) * lg_ref[...] + lb_ref[...]
        z = _silu(y).astype(_BF)
        x1 = x_ref[...] + gate1_ref[...] * (_dot(z, w2_ref[...]) + b2_ref[...])
        o_ref[...] = x1
        h_ref[...] = _rms_mod(x1, g_ref[...], sh_ref[...], sc_ref[...]).astype(_BF)

    _ffn_step(h_ref, gate2_ref, wg_ref, wu_ref, wd_ref, fg_ref, o_ref, final_norm)


def _mlstm_out_ffn_kernel(z_ref, w2_ref, x_ref, gate1_ref, g_ref, sh_ref, sc_ref,
                          gate2_ref, wg_ref, wu_ref, wd_ref, fg_ref, o_ref, h_ref, *, final_norm):
    @pl.when(pl.program_id(1) == 0)
    def _():
        x1 = x_ref[...] + gate1_ref[...] * _dot(z_ref[...], w2_ref[...])
        o_ref[...] = x1
        h_ref[...] = _rms_mod(x1, g_ref[...], sh_ref[...], sc_ref[...]).astype(_BF)

    _ffn_step(h_ref, gate2_ref, wg_ref, wu_ref, wd_ref, fg_ref, o_ref, final_norm)


def _mix_out_ffn(st, mix_args, mix_specs, kern, xs, mods, layer, g2, w_gu, w_down, final_g, final_norm):
    d, tm = st.d, st.tm
    ff = w_down.shape[0]
    tf = _tile(ff, COL_TILE)
    nf = ff // tf
    row_block = pl.BlockSpec((tm, d), lambda i, f: (i, 0))
    return pl.pallas_call(
        functools.partial(kern, final_norm=final_norm),
        grid=(st.tiles, nf),
        in_specs=mix_specs + [
            row_block,
            st.mod_spec(layer, 2),
            _resident((1, d)),
            st.mod_spec(layer, 3),
            st.mod_spec(layer, 4),
            st.mod_spec(layer, 5),
            w_gu.tiled((d, tf), lambda i, f: (0, f)),
            w_gu.tiled((d, tf), lambda i, f: (0, nf + f)),
            w_down.tiled((tf, d), lambda i, f: (f, 0)),
            _resident((1, d)),
        ],
        out_specs=row_block,
        out_shape=jax.ShapeDtypeStruct((st.n, d), _F32),
        scratch_shapes=[pltpu.VMEM((tm, d), _BF)],
        compiler_params=_params("parallel", "arbitrary"),
        name=kern.__name__.strip("_").replace("_kernel", ""),
    )(*mix_args, xs, mods, g2.reshape(1, d), mods, mods, mods, w_gu.arr, w_gu.arr, w_down.arr,
      final_g.reshape(1, d))


def _conv_out_ffn(st, v, xs, mods, layer, ln_g, ln_b, w_pw2, b_pw2, g2, w_gu, w_down, final_g, final_norm):
    d = st.d
    specs = [pl.BlockSpec((st.tm, d), lambda i, f: (i, 0)), _resident((1, d)), _resident((1, d)),
             w_pw2.resident(), _resident((1, d))]
    args = (v, ln_g.reshape(1, d), ln_b.reshape(1, d), w_pw2.arr, b_pw2.reshape(1, d))
    return _mix_out_ffn(st, args, specs, _conv_out_ffn_kernel, xs, mods, layer, g2, w_gu, w_down,
                        final_g, final_norm)


def _mlstm_out_ffn(st, z, xs, mods, layer, w_out, g2, w_gu, w_down, final_g, final_norm):
    d = st.d
    specs = [pl.BlockSpec((st.tm, d), lambda i, f: (i, 0)), w_out.resident()]
    return _mix_out_ffn(st, (z, w_out.arr), specs, _mlstm_out_ffn_kernel, xs, mods, layer, g2, w_gu, w_down,
                        final_g, final_norm)


def _min_proj_kernel(x_ref, g_ref, sh_ref, sc_ref, w_ref, wg_ref, cs_ref, p_ref, gt_ref, h_ref, *, tn, k_cols):
    h_ref[...] = _rms_mod(x_ref[...], g_ref[...], sh_ref[...], sc_ref[...]).astype(_BF)
    gt_ref[...] = _dot(h_ref[...], wg_ref[...])
    for j in range(p_ref.shape[1] // tn):
        lo, hi = j * tn, (j + 1) * tn
        acc = _dot(h_ref[...], w_ref[:, lo:hi])
        if lo < k_cols[1] and hi > k_cols[0]:
            acc = acc * cs_ref[:, lo:hi]
        p_ref[:, lo:hi] = acc.astype(_BF)


def _min_proj(st, xs, mods, layer, g1, w_in, n_main, w_gates, col_scale, k_cols):
    d, tm = st.d, st.tm
    return pl.pallas_call(
        functools.partial(_min_proj_kernel, tn=_tile(n_main, COL_TILE), k_cols=k_cols),
        grid=(st.tiles,),
        in_specs=[
            pl.BlockSpec((tm, d), lambda i: (i, 0)),
            _resident((1, d)),
            st.mod_spec(layer, 0),
            st.mod_spec(layer, 1),
            w_in.resident((d, n_main)),
            w_gates.resident(),
            _resident((1, n_main)),
        ],
        out_specs=[
            pl.BlockSpec((tm, n_main), lambda i: (i, 0)),
            pl.BlockSpec((tm, LANES), lambda i: (i, 0)),
        ],
        out_shape=[
            jax.ShapeDtypeStruct((st.n, n_main), _BF),
            jax.ShapeDtypeStruct((st.n, LANES), _F32),
        ],
        scratch_shapes=[pltpu.VMEM((tm, d), _BF)],
        compiler_params=_params("parallel"),
        name="mlstm_in_proj",
    )(xs, g1.reshape(1, d), mods, mods, w_in.arr, w_gates.arr, col_scale)


def _log_sigmoid(v):
    return jnp.minimum(v, 0.0) - jnp.log(1.0 + jnp.exp(-jnp.abs(v)))


def _gate_prep_kernel(g_ref, b_ref, col_ref, row_ref, *, heads, chunk):
    rows = g_ref.shape[0]
    g = g_ref[...] + b_ref[...]
    g = GATE_CAP * jnp.tanh(g / GATE_CAP)
    lane = lax.broadcasted_iota(jnp.int32, g.shape, 1)
    is_f = (lane // heads) % 2 == 1
    vals = jnp.where(is_f, _log_sigmoid(g), g)
    t = lax.broadcasted_iota(jnp.int32, g.shape, 0) % chunk
    pre = vals
    suf = vals
    shift = 1
    while shift < chunk:
        pre = pre + jnp.where(t >= shift, pltpu.roll(pre, shift, 0), 0.0)
        suf = suf + jnp.where(t + shift < chunk, pltpu.roll(suf, rows - shift, 0), 0.0)
        shift *= 2
    lane_o = lax.broadcasted_iota(jnp.int32, (rows, LANES), 1)
    for h in range(heads):
        li_f = vals[:, h:h + 1]
        b_f = pre[:, heads + h:heads + h + 1]
        li_b = vals[:, 2 * heads + h:2 * heads + h + 1]
        b_b = suf[:, 3 * heads + h:3 * heads + h + 1]
        blk = jnp.where(lane_o == 0, li_f, jnp.where(lane_o == 1, b_f, jnp.where(lane_o == 2, li_b, b_b)))
        col_ref[:, pl.ds(h * LANES, LANES)] = blk
        row_ref[pl.ds(h * SUBLANES, SUBLANES), :] = blk.T[0:SUBLANES, :]


def _gate_prep(st, gates, b_gates, heads, chunk):
    n, tr = st.n, st.tm
    assert tr % chunk == 0
    bias = jnp.zeros((1, LANES), _F32).at[0, :4 * heads].set(b_gates)
    return pl.pallas_call(
        functools.partial(_gate_prep_kernel, heads=heads, chunk=chunk),
        grid=(n // tr,),
        in_specs=[
            pl.BlockSpec((tr, LANES), lambda i: (i, 0)),
            pl.BlockSpec((1, LANES), lambda i: (0, 0)),
        ],
        out_specs=[
            pl.BlockSpec((tr, heads * LANES), lambda i: (i, 0)),
            pl.BlockSpec((heads * SUBLANES, tr), lambda i: (0, i)),
        ],
        out_shape=[
            jax.ShapeDtypeStruct((n, heads * LANES), _F32),
            jax.ShapeDtypeStruct((heads * SUBLANES, n), _F32),
        ],
        compiler_params=_params("parallel"),
        name="mlstm_gate_prep",
    )(gates, bias)


def _scan_kernel(qx_ref, kx_ref, vx_ref, ox_ref, gcx_ref, grx_ref,
                 qc_ref, kc_ref, vc_ref, oc_ref, gcc_ref, grc_ref, hg_ref,
                 zx_ref, zc_ref,
                 cf_ref, mf_ref, cb_ref, mb_ref, vax_ref, vac_ref, hfx_ref, hbx_ref, hfc_ref, hbc_ref, *, chunk):
    T = chunk
    dv = vx_ref.shape[1]
    dqk = qx_ref.shape[1]
    row_i = lax.broadcasted_iota(jnp.int32, (T, T), 0)
    col_i = lax.broadcasted_iota(jnp.int32, (T, T), 1)

    for ref in (cf_ref, mf_ref, cb_ref, mb_ref):
        ref[...] = jnp.zeros_like(ref)
    for v_ref, va_ref in ((vx_ref, vax_ref), (vc_ref, vac_ref)):
        va_ref[:, :dv] = v_ref[...]
        va_ref[:, dv:] = jnp.ones((va_ref.shape[0], LANES), _BF)

    def step(q_ref, k_ref, va_ref, gc_ref, gr_ref, r0, c_ref, m_ref, backward):
        rows = slice(r0, r0 + T)
        q = q_ref[rows, :]
        k = k_ref[rows, :]
        va = va_ref[rows, :]
        gcb = gc_ref[rows, :]
        grb = gr_ref[:, rows]
        o = 2 if backward else 0
        li_rep = jnp.broadcast_to(gcb[:, o:o + 1], (T, LANES))
        b_rep = jnp.broadcast_to(gcb[:, o + 1:o + 2], (T, LANES))
        li_r, b_r = grb[o:o + 1, :], grb[o + 1:o + 2, :]
        m = m_ref[0:1, :]
        a_rep = b_rep + m
        mt_rep = jnp.maximum(a_rep, jnp.max(li_r, axis=1, keepdims=True))
        w_inter = jnp.exp(a_rep - mt_rep)
        d = _lane_tile(b_rep - mt_rep, T) + (li_r - b_r)
        p = jnp.exp(jnp.where((row_i <= col_i) if backward else (row_i >= col_i), d, -jnp.inf))
        qk = lax.dot_general(q, k, (((1,), (1,)), ((), ())), preferred_element_type=_F32)
        s = (qk * p).astype(_BF)
        c_old = c_ref[...]
        num = _lane_tile(w_inter, dv + LANES) * _dot(q, c_old.astype(_BF)) + _dot(s, va)
        den = num[:, dv:]
        inv = 1.0 / jnp.maximum(jnp.abs(den), jnp.exp(-mt_rep))
        h = num[:, :dv] * _lane_tile(inv, dv)
        b_tot = b_rep[0:1, :] if backward else b_rep[T - 1:T, :]
        gl = b_tot - b_rep + li_rep
        m_new = jnp.maximum(b_tot + m, jnp.max(gl, axis=0, keepdims=True))
        decay = jnp.exp(b_tot + m - m_new)
        kw = (k.astype(_F32) * _lane_tile(jnp.exp(gl - m_new), dqk)).astype(_BF)
        c_ref[...] = _lane_tile(decay, dv + LANES) * c_old + lax.dot_general(
            kw, va, (((0,), (0,)), ((), ())), preferred_element_type=_F32)
        m_ref[...] = jnp.broadcast_to(m_new, m_ref.shape)
        return h

    def run(q_ref, k_ref, va_ref, o_ref, gc_ref, gr_ref, hf_ref, hb_ref, z_ref):
        n_chunks = q_ref.shape[0] // T

        def finalize(ci):
            rows = slice(ci * T, (ci + 1) * T)
            hs = hf_ref[rows, :] + hb_ref[rows, :]
            hn = hs * lax.rsqrt(jnp.mean(hs * hs, axis=-1, keepdims=True) + EPS)
            gate = _sigmoid(o_ref[rows, :].astype(_F32))
            z_ref[rows, :] = (hn * hg_ref[...] * gate).astype(_BF)

        for i in range(n_chunks):
            jf, jb = i, n_chunks - 1 - i
            hf_ref[jf * T:(jf + 1) * T, :] = step(q_ref, k_ref, va_ref, gc_ref, gr_ref, jf * T, cf_ref, mf_ref, False)
            hb_ref[jb * T:(jb + 1) * T, :] = step(q_ref, k_ref, va_ref, gc_ref, gr_ref, jb * T, cb_ref, mb_ref, True)
            if jf >= jb:
                for ci in sorted({jf, jb}):
                    finalize(ci)

    run(qc_ref, kc_ref, vac_ref, oc_ref, gcc_ref, grc_ref, hfc_ref, hbc_ref, zc_ref)
    run(qx_ref, kx_ref, vax_ref, ox_ref, gcx_ref, grx_ref, hfx_ref, hbx_ref, zx_ref)


def _scan(batch, seq, ctx_len, d, proj_x, gcol_x, grow_x, proj_c, gcol_c, grow_c, hn_g, heads, chunk):
    dqk_all = (proj_x.shape[1] - 2 * d) // 2
    dqk, dv = dqk_all // heads, d // heads
    assert (2 * dqk_all) % dv == 0 and seq % chunk == 0 and ctx_len % chunk == 0
    assert chunk % LANES == 0 and dv % LANES == 0
    kq, vq, oq = heads, 2 * dqk_all // dv, (2 * dqk_all + d) // dv

    def stream_specs(length):
        def blk(width, off):
            return pl.BlockSpec((length, width), lambda b, h: (b, off + h))
        return [blk(dqk, 0), blk(dqk, kq), blk(dv, vq), blk(dv, oq),
                pl.BlockSpec((length, LANES), lambda b, h: (b, h)),
                pl.BlockSpec((SUBLANES, length), lambda b, h: (h, b))]

    return pl.pallas_call(
        functools.partial(_scan_kernel, chunk=chunk),
        grid=(batch, heads),
        in_specs=stream_specs(seq) + stream_specs(ctx_len) + [pl.BlockSpec((1, dv), lambda b, h: (0, h))],
        out_specs=[
            pl.BlockSpec((seq, dv), lambda b, h: (b, h)),
            pl.BlockSpec((ctx_len, dv), lambda b, h: (b, h)),
        ],
        out_shape=[
            jax.ShapeDtypeStruct((batch * seq, d), _BF),
            jax.ShapeDtypeStruct((batch * ctx_len, d), _BF),
        ],
        scratch_shapes=[
            pltpu.VMEM((dqk, dv + LANES), _F32), pltpu.VMEM((SUBLANES, LANES), _F32),
            pltpu.VMEM((dqk, dv + LANES), _F32), pltpu.VMEM((SUBLANES, LANES), _F32),
            pltpu.VMEM((seq, dv + LANES), _BF), pltpu.VMEM((ctx_len, dv + LANES), _BF),
            pltpu.VMEM((seq, dv), _F32), pltpu.VMEM((seq, dv), _F32),
            pltpu.VMEM((ctx_len, dv), _F32), pltpu.VMEM((ctx_len, dv), _F32),
        ],
        compiler_params=_params("parallel", "parallel"),
        name="mlstm_scan",
    )(proj_x, proj_x, proj_x, proj_x, gcol_x, grow_x,
      proj_c, proj_c, proj_c, proj_c, gcol_c, grow_c, hn_g.reshape(1, d))


def kernel(x, c, ctx, c_ctx, norm1_g, norm2_g, w_mod, b_mod, w_gu, w_down, conv_w_pw1, conv_b_pw1,
           conv_w_dw, conv_b_dw, conv_ln_g, conv_ln_b, conv_w_pw2, conv_b_pw2, m_w_in, m_b_gates,
           m_hn_g, m_w_out, final_g):
    batch, seq, d = x.shape
    ctx_len = ctx.shape[1]
    depth = w_mod.shape[0]
    heads = max(4, d // 512)
    dqk_all = d // 2
    chunk = _tile(ctx_len, 256)
    assert batch + 1 <= MOD_ROWS and 4 * heads <= LANES

    sx = _Stream(batch * seq, d, None)
    assert seq % sx.tm == 0
    tiles_per_sample = seq // sx.tm
    sx.mod_row = lambda i: i // tiles_per_sample
    sc = _Stream(batch * ctx_len, d, lambda i: batch)

    cc = jnp.zeros((MOD_ROWS, d), _F32).at[:batch].set(c).at[batch].set(c_ctx)
    mods = _modulation(cc, w_mod, b_mod).reshape(depth, MOD_ROWS, 6, 1, d)

    w_gu_b, w_down_b = w_gu.astype(_BF), w_down.astype(_BF)
    w_pw1_b, w_pw2_b = conv_w_pw1.astype(_BF), conv_w_pw2.astype(_BF)
    n_main = 2 * dqk_all + 2 * d
    w_in_b = m_w_in.astype(_BF)
    w_gates_b = jnp.zeros((m_w_in.shape[0], d, LANES), _BF).at[:, :, :4 * heads].set(
        m_w_in[:, :, n_main:].astype(_BF))
    w_out_b = m_w_out.astype(_BF)
    k_cols = (dqk_all, 2 * dqk_all)
    col_scale = jnp.ones((1, n_main), _F32).at[:, k_cols[0]:k_cols[1]].set(float(dqk_all // heads) ** -0.5)

    xs = x.reshape(batch * seq, d)
    cs = ctx.reshape(batch * ctx_len, d)
    for i in range(depth):
        last = i == depth - 1
        j = i // 2
        ffn_w = (norm2_g[i], _Layered(w_gu_b, i), _Layered(w_down_b, i), final_g)
        if i % 2 == 0:
            conv_w = (conv_ln_g[j], conv_ln_b[j], _Layered(w_pw2_b, j), conv_b_pw2[j])
            w_pw1 = _Layered(w_pw1_b, j)
            ux = _pw1_glu(sx, xs, mods, i, norm1_g[i], w_pw1, conv_b_pw1[j])
            vx = _dwconv(ux, batch, seq, True, conv_w_dw[j], conv_b_dw[j])
            if not last:
                uc = _pw1_glu(sc, cs, mods, i, norm1_g[i], w_pw1, conv_b_pw1[j])
                vc = _dwconv(uc, batch, ctx_len, False, conv_w_dw[j], conv_b_dw[j])
                cs = _conv_out_ffn(sc, vc, cs, mods, i, *conv_w, *ffn_w, False)
            xs = _conv_out_ffn(sx, vx, xs, mods, i, *conv_w, *ffn_w, last)
        else:
            in_w = (_Layered(w_in_b, j), n_main, _Layered(w_gates_b, j), col_scale, k_cols)
            w_out = _Layered(w_out_b, j)
            px, gx = _min_proj(sx, xs, mods, i, norm1_g[i], *in_w)
            pc, gc = _min_proj(sc, cs, mods, i, norm1_g[i], *in_w)
            gcol_x, grow_x = _gate_prep(sx, gx, m_b_gates[j], heads, chunk)
            gcol_c, grow_c = _gate_prep(sc, gc, m_b_gates[j], heads, chunk)
            zx, zc = _scan(batch, seq, ctx_len, d, px, gcol_x, grow_x, pc, gcol_c, grow_c,
                           m_hn_g[j], heads, chunk)
            if not last:
                cs = _mlstm_out_ffn(sc, zc, cs, mods, i, w_out, *ffn_w, False)
            xs = _mlstm_out_ffn(sx, zx, xs, mods, i, w_out, *ffn_w, last)
    return xs.reshape(batch, seq, d)
```

```python
import functools

import jax
import jax.numpy as jnp
from jax import lax
from jax.experimental import pallas as pl
from jax.experimental.pallas import tpu as pltpu

EPS = 1e-6
GATE_CAP = 15.0
GRID_W = 64
LANES = 128
SUBLANES = 8
VMEM_LIMIT = 56 * 1024 * 1024
MOD_ROWS = 32
ROW_TILE = 512
COL_TILE = 512

_BF = jnp.bfloat16
_F32 = jnp.float32


def _tile(dim, pref):
    t = min(dim, pref)
    while dim % t:
        t //= 2
    return t


def _params(*sem):
    return pltpu.CompilerParams(dimension_semantics=sem, vmem_limit_bytes=VMEM_LIMIT)


def _sigmoid(v):
    return 0.5 * (1.0 + jnp.tanh(0.5 * v))


def _silu(v):
    h = 0.5 * v
    return h + h * jnp.tanh(h)


def _lane_tile(rep, width):
    if width % LANES == 0:
        return rep if width == LANES else jnp.concatenate([rep] * (width // LANES), axis=1)
    assert width < LANES
    return rep[:, :width]


def _dot(a, b):
    return jnp.dot(a, b, preferred_element_type=_F32)


def _rms_mod(x, g, shift, scale):
    y = x * lax.rsqrt(jnp.mean(x * x, axis=-1, keepdims=True) + EPS)
    return y * (g * (1.0 + scale)) + shift


def _resident(shape):
    zeros = (0,) * len(shape)
    return pl.BlockSpec(shape, lambda *_: zeros, pipeline_mode=pl.Buffered(1))


class _Layered:
    def __init__(self, stacked, index):
        self.arr, self.index = stacked, index
        self.shape = stacked.shape[1:]

    def resident(self, shape=None):
        shape = self.shape if shape is None else shape
        index = (self.index,) + (0,) * len(shape)
        return pl.BlockSpec((None,) + tuple(shape), lambda *_: index, pipeline_mode=pl.Buffered(1))

    def tiled(self, block, index_map):
        return pl.BlockSpec((None,) + tuple(block), lambda *a: (self.index,) + tuple(index_map(*a)))


def _mod_kernel(cc_ref, w_ref, b_ref, o_ref):
    cc = cc_ref[...]
    s = _silu(cc).astype(_BF)
    o_ref[...] = _dot(s, w_ref[...].astype(_BF)) + b_ref[...]


def _modulation(cc, w_mod, b_mod):
    depth, d, n = w_mod.shape
    tn = _tile(n, 1024)
    return pl.pallas_call(
        _mod_kernel,
        grid=(depth, n // tn),
        in_specs=[
            pl.BlockSpec((MOD_ROWS, d), lambda l, j: (0, 0)),
            pl.BlockSpec((None, d, tn), lambda l, j: (l, 0, j)),
            pl.BlockSpec((None, 1, tn), lambda l, j: (l, 0, j)),
        ],
        out_specs=pl.BlockSpec((None, MOD_ROWS, tn), lambda l, j: (l, 0, j)),
        out_shape=jax.ShapeDtypeStruct((depth, MOD_ROWS, n), _F32),
        compiler_params=_params("parallel", "parallel"),
        name="modulation",
    )(cc, w_mod, b_mod.reshape(depth, 1, n))


class _Stream:
    def __init__(self, n_rows, d, mod_row):
        self.n, self.d = n_rows, d
        self.tm = _tile(n_rows, ROW_TILE)
        self.tiles = n_rows // self.tm
        self.mod_row = mod_row

    def mod_spec(self, layer, part):
        return pl.BlockSpec((None, None, None, 1, self.d),
                            lambda i, *_: (layer, self.mod_row(i), part, 0, 0))


def _pw1_glu_kernel(x_ref, g_ref, sh_ref, sc_ref, w_ref, b_ref, u_ref, h_ref, *, tn):
    h_ref[...] = _rms_mod(x_ref[...], g_ref[...], sh_ref[...], sc_ref[...]).astype(_BF)
    d = h_ref.shape[1]
    for j in range(d // tn):
        lo, hi = j * tn, (j + 1) * tn
        a = _dot(h_ref[...], w_ref[:, lo:hi]) + b_ref[:, lo:hi]
        b = _dot(h_ref[...], w_ref[:, d + lo:d + hi]) + b_ref[:, d + lo:d + hi]
        u_ref[:, lo:hi] = (a * _sigmoid(b)).astype(u_ref.dtype)


def _pw1_glu(st, xs, mods, layer, g1, w_pw1, b_pw1):
    d, tm = st.d, st.tm
    return pl.pallas_call(
        functools.partial(_pw1_glu_kernel, tn=_tile(d, COL_TILE)),
        grid=(st.tiles,),
        in_specs=[
            pl.BlockSpec((tm, d), lambda i: (i, 0)),
            _resident((1, d)),
            st.mod_spec(layer, 0),
            st.mod_spec(layer, 1),
            w_pw1.resident(),
            _resident((1, 2 * d)),
        ],
        out_specs=pl.BlockSpec((tm, d), lambda i: (i, 0)),
        out_shape=jax.ShapeDtypeStruct((st.n, d), _BF),
        scratch_shapes=[pltpu.VMEM((tm, d), _BF)],
        compiler_params=_params("parallel"),
        name="pw1_glu",
    )(xs, g1.reshape(1, d), mods, mods, w_pw1.arr, b_pw1.reshape(1, 2 * d))


_CONV_LPAD = 16


def _conv_along_sublanes(pad_ref, w_ref, b_ref, n_rows, row_len, taps, store):
    tc = pad_ref.shape[-1]
    first = _CONV_LPAD - taps // 2

    def body(r, carry):
        for l0 in range(0, tc, LANES):
            lanes = pl.ds(l0, LANES)
            acc = jnp.zeros((row_len, LANES), _F32)
            for s in range(SUBLANES):
                ks = [k for k in range(taps) if (first + k) % SUBLANES == s]
                if not ks:
                    continue
                lo, hi = first + ks[0], first + ks[-1]
                n = row_len + hi - lo + (SUBLANES if s else 0)
                span = pad_ref[r, pl.ds(lo - s, n), lanes]
                if s:
                    span = pltpu.roll(span, n - s, 0)
                for k in ks:
                    off = first + k - lo
                    acc = acc + span[off:off + row_len] * w_ref[pl.ds(k, 1), lanes]
            store(r, l0, acc + b_ref[:, lanes])
        return carry
    lax.fori_loop(0, n_rows, body, 0)


def _dwconv_grid_kernel(u_ref, w_ref, b_ref, o_ref, padh_ref, padv_ref, *, n_h_tiles, taps):
    rows, _, tc = u_ref.shape
    half = taps // 2

    def store(r, l0, val):
        o_ref[r, :, pl.ds(l0, LANES)] = val.astype(o_ref.dtype)

    @pl.when(pl.program_id(1) < n_h_tiles)
    def _():
        zeros = jnp.zeros((rows, _CONV_LPAD, tc), _F32)
        padh_ref[:, pl.ds(0, _CONV_LPAD), :] = zeros
        padh_ref[:, pl.ds(_CONV_LPAD + GRID_W, _CONV_LPAD), :] = zeros
        padh_ref[:, pl.ds(_CONV_LPAD, GRID_W), :] = u_ref[...].astype(_F32)
        _conv_along_sublanes(padh_ref, w_ref, b_ref, rows, GRID_W, taps, store)

    @pl.when(pl.program_id(1) >= n_h_tiles)
    def _():
        zeros = jnp.zeros((half, GRID_W, tc), _F32)
        padv_ref[pl.ds(0, half)] = zeros
        padv_ref[pl.ds(half + rows, half)] = zeros
        padv_ref[pl.ds(half, rows)] = u_ref[...].astype(_F32)

        def body(r, carry):
            for l0 in range(0, tc, LANES):
                lanes = pl.ds(l0, LANES)
                acc = jnp.zeros((GRID_W, LANES), _F32)
                for k in range(taps):
                    acc = acc + padv_ref[r + k, :, lanes] * w_ref[pl.ds(k, 1), lanes]
                store(r, l0, acc + b_ref[:, lanes])
            return carry
        lax.fori_loop(0, rows, body, 0)


def _dwconv_seq_kernel(u_ref, w_ref, b_ref, o_ref, pad_ref, *, taps):
    n_seq, seq_len, tc = u_ref.shape
    zeros = jnp.zeros((n_seq, _CONV_LPAD, tc), _F32)
    pad_ref[:, pl.ds(0, _CONV_LPAD), :] = zeros
    pad_ref[:, pl.ds(_CONV_LPAD + seq_len, _CONV_LPAD), :] = zeros
    pad_ref[:, pl.ds(_CONV_LPAD, seq_len), :] = u_ref[...].astype(_F32)

    def store(r, l0, val):
        o_ref[r, :, pl.ds(l0, LANES)] = val.astype(o_ref.dtype)
    _conv_along_sublanes(pad_ref, w_ref, b_ref, n_seq, seq_len, taps, store)


def _dwconv(u, batch, seq, on_grid, w_dw, b_dw):
    n, d = u.shape
    taps = w_dw.shape[0]
    assert taps // 2 <= _CONV_LPAD
    tc = _tile(d // 2, 256)
    wspecs = [pl.BlockSpec((taps, tc), lambda i, c: (0, c)), pl.BlockSpec((1, tc), lambda i, c: (0, c))]
    if on_grid:
        assert seq % GRID_W == 0
        rows = seq // GRID_W
        block = pl.BlockSpec((rows, GRID_W, tc), lambda i, c: (i, 0, c))
        out = pl.pallas_call(
            functools.partial(_dwconv_grid_kernel, n_h_tiles=(d // 2) // tc, taps=taps),
            grid=(batch, d // tc),
            in_specs=[block] + wspecs,
            out_specs=block,
            out_shape=jax.ShapeDtypeStruct((n // GRID_W, GRID_W, d), u.dtype),
            scratch_shapes=[
                pltpu.VMEM((rows, GRID_W + 2 * _CONV_LPAD, tc), _F32),
                pltpu.VMEM((rows + 2 * (taps // 2), GRID_W, tc), _F32),
            ],
            compiler_params=_params("parallel", "parallel"),
            name="dwconv_grid",
        )(u.reshape(n // GRID_W, GRID_W, d), w_dw, b_dw.reshape(1, d))
    else:
        bb = _tile(batch, 8)
        block = pl.BlockSpec((bb, seq, tc), lambda i, c: (i, 0, c))
        out = pl.pallas_call(
            functools.partial(_dwconv_seq_kernel, taps=taps),
            grid=(batch // bb, d // tc),
            in_specs=[block] + wspecs,
            out_specs=block,
            out_shape=jax.ShapeDtypeStruct((batch, seq, d), u.dtype),
            scratch_shapes=[pltpu.VMEM((bb, seq + 2 * _CONV_LPAD, tc), _F32)],
            compiler_params=_params("parallel", "parallel"),
            name="dwconv_seq",
        )(u.reshape(batch, seq, d), w_dw, b_dw.reshape(1, d))
    return out.reshape(n, d)


def _ffn_step(h_ref, gate2_ref, wg_ref, wu_ref, wd_ref, fg_ref, o_ref, final_norm):
    h = h_ref[...]
    g = _dot(h, wg_ref[...])
    u = _dot(h, wu_ref[...])
    act = (_silu(g) * u).astype(_BF)
    o_ref[...] += gate2_ref[...] * _dot(act, wd_ref[...])
    if final_norm:
        @pl.when(pl.program_id(1) == pl.num_programs(1) - 1)
        def _():
            y = o_ref[...]
            o_ref[...] = y * lax.rsqrt(jnp.mean(y * y, axis=-1, keepdims=True) + EPS) * fg_ref[...]


def _conv_out_ffn_kernel(v_ref, lg_ref, lb_ref, w2_ref, b2_ref, x_ref, gate1_ref, g_ref, sh_ref, sc_ref,
                         gate2_ref, wg_ref, wu_ref, wd_ref, fg_ref, o_ref, h_ref, *, final_norm):
    @pl.when(pl.program_id(1) == 0)
    def _():
        v = v_ref[...].astype(_F32)
        mu = jnp.mean(v, axis=-1, keepdims=True)
        vc = v - mu
        var = jnp.mean(vc * vc, axis=-1, keepdims=True)
        y = vc * lax.rsqrt(var + EPS) * lg_ref[...] + lb_ref[...]
        z = _silu(y).astype(_BF)
        x1 = x_ref[...] + gate1_ref[...] * (_dot(z, w2_ref[...]) + b2_ref[...])
        o_ref[...] = x1
        h_ref[...] = _rms_mod(x1, g_ref[...], sh_ref[...], sc_ref[...]).astype(_BF)

    _ffn_step(h_ref, gate2_ref, wg_ref, wu_ref, wd_ref, fg_ref, o_ref, final_norm)


def _mlstm_out_ffn_kernel(z_ref, w2_ref, x_ref, gate1_ref, g_ref, sh_ref, sc_ref,
                          gate2_ref, wg_ref, wu_ref, wd_ref, fg_ref, o_ref, h_ref, *, final_norm):
    @pl.when(pl.program_id(1) == 0)
    def _():
        x1 = x_ref[...] + gate1_ref[...] * _dot(z_ref[...], w2_ref[...])
        o_ref[...] = x1
        h_ref[...] = _rms_mod(x1, g_ref[...], sh_ref[...], sc_ref[...]).astype(_BF)

    _ffn_step(h_ref, gate2_ref, wg_ref, wu_ref, wd_ref, fg_ref, o_ref, final_norm)


def _mix_out_ffn(st, mix_args, mix_specs, kern, xs, mods, layer, g2, w_gu, w_down, final_g, final_norm):
    d, tm = st.d, st.tm
    _, nf, _, tf = w_gu.shape
    assert w_down.shape == (nf * tf, d)
    row_block = pl.BlockSpec((tm, d), lambda i, f: (i, 0))
    return pl.pallas_call(
        functools.partial(kern, final_norm=final_norm),
        grid=(st.tiles, nf),
        in_specs=mix_specs + [
            row_block,
            st.mod_spec(layer, 2),
            _resident((1, d)),
            st.mod_spec(layer, 3),
            st.mod_spec(layer, 4),
            st.mod_spec(layer, 5),
            w_gu.tiled((None, None, d, tf), lambda i, f: (0, f, 0, 0)),
            w_gu.tiled((None, None, d, tf), lambda i, f: (1, f, 0, 0)),
            w_down.tiled((tf, d), lambda i, f: (f, 0)),
            _resident((1, d)),
        ],
        out_specs=row_block,
        out_shape=jax.ShapeDtypeStruct((st.n, d), _F32),
        scratch_shapes=[pltpu.VMEM((tm, d), _BF)],
        compiler_params=_params("parallel", "arbitrary"),
        name=kern.__name__.strip("_").replace("_kernel", ""),
    )(*mix_args, xs, mods, g2.reshape(1, d), mods, mods, mods, w_gu.arr, w_gu.arr, w_down.arr,
      final_g.reshape(1, d))


def _conv_out_ffn(st, v, xs, mods, layer, ln_g, ln_b, w_pw2, b_pw2, g2, w_gu, w_down, final_g, final_norm):
    d = st.d
    specs = [pl.BlockSpec((st.tm, d), lambda i, f: (i, 0)), _resident((1, d)), _resident((1, d)),
             w_pw2.resident(), _resident((1, d))]
    args = (v, ln_g.reshape(1, d), ln_b.reshape(1, d), w_pw2.arr, b_pw2.reshape(1, d))
    return _mix_out_ffn(st, args, specs, _conv_out_ffn_kernel, xs, mods, layer, g2, w_gu, w_down,
                        final_g, final_norm)


def _mlstm_out_ffn(st, z, xs, mods, layer, w_out, g2, w_gu, w_down, final_g, final_norm):
    d = st.d
    specs = [pl.BlockSpec((st.tm, d), lambda i, f: (i, 0)), w_out.resident()]
    return _mix_out_ffn(st, (z, w_out.arr), specs, _mlstm_out_ffn_kernel, xs, mods, layer, g2, w_gu, w_down,
                        final_g, final_norm)


def _min_proj_kernel(x_ref, g_ref, sh_ref, sc_ref, w_ref, wg_ref, q_ref, k_ref, v_ref, o_ref, gt_ref, h_ref, *, tn):
    h_ref[...] = _rms_mod(x_ref[...], g_ref[...], sh_ref[...], sc_ref[...]).astype(_BF)
    gt_ref[...] = _dot(h_ref[...], wg_ref[...])
    heads, _, dqk = q_ref.shape
    dv = v_ref.shape[2]
    k_scale = float(dqk) ** -0.5
    pieces, start = [], 0
    for ref, width, scale in ((q_ref, dqk, None), (k_ref, dqk, k_scale), (v_ref, dv, None), (o_ref, dv, None)):
        for h in range(heads):
            pieces.append((start, width, ref, h, scale))
            start += width
    for lo in range(0, start, tn):
        hi = lo + tn
        acc = _dot(h_ref[...], w_ref[:, lo:hi])
        for ps, width, ref, h, scale in pieces:
            a, b = max(lo, ps), min(hi, ps + width)
            if a < b:
                part = acc[:, a - lo:b - lo]
                ref[h, :, a - ps:b - ps] = (part if scale is None else part * scale).astype(_BF)


def _min_proj(st, xs, mods, layer, g1, w_in, heads, dqk, dv, w_gates):
    d, tm = st.d, st.tm
    n_main = heads * (2 * dqk + 2 * dv)

    def head_major(width):
        return (pl.BlockSpec((heads, tm, width), lambda i: (0, i, 0)),
                jax.ShapeDtypeStruct((heads, st.n, width), _BF))
    outs = [head_major(dqk), head_major(dqk), head_major(dv), head_major(dv),
            (pl.BlockSpec((tm, LANES), lambda i: (i, 0)), jax.ShapeDtypeStruct((st.n, LANES), _F32))]
    return pl.pallas_call(
        functools.partial(_min_proj_kernel, tn=_tile(n_main, COL_TILE)),
        grid=(st.tiles,),
        in_specs=[
            pl.BlockSpec((tm, d), lambda i: (i, 0)),
            _resident((1, d)),
            st.mod_spec(layer, 0),
            st.mod_spec(layer, 1),
            w_in.resident((d, n_main)),
            w_gates.resident(),
        ],
        out_specs=[o[0] for o in outs],
        out_shape=[o[1] for o in outs],
        scratch_shapes=[pltpu.VMEM((tm, d), _BF)],
        compiler_params=_params("parallel"),
        name="mlstm_in_proj",
    )(xs, g1.reshape(1, d), mods, mods, w_in.arr, w_gates.arr)


def _log_sigmoid(v):
    return jnp.minimum(v, 0.0) - jnp.log(1.0 + jnp.exp(-jnp.abs(v)))


def _gate_prep_kernel(g_ref, b_ref, col_ref, row_ref, *, heads, chunk):
    rows = g_ref.shape[0]
    g = g_ref[...] + b_ref[...]
    g = GATE_CAP * jnp.tanh(g / GATE_CAP)
    lane = lax.broadcasted_iota(jnp.int32, g.shape, 1)
    is_f = (lane // heads) % 2 == 1
    vals = jnp.where(is_f, _log_sigmoid(g), g)
    t = lax.broadcasted_iota(jnp.int32, g.shape, 0) % chunk
    pre = vals
    suf = vals
    shift = 1
    while shift < chunk:
        pre = pre + jnp.where(t >= shift, pltpu.roll(pre, shift, 0), 0.0)
        suf = suf + jnp.where(t + shift < chunk, pltpu.roll(suf, rows - shift, 0), 0.0)
        shift *= 2
    lane_o = lax.broadcasted_iota(jnp.int32, (rows, LANES), 1)
    for h in range(heads):
        li_f = vals[:, h:h + 1]
        b_f = pre[:, heads + h:heads + h + 1]
        li_b = vals[:, 2 * heads + h:2 * heads + h + 1]
        b_b = suf[:, 3 * heads + h:3 * heads + h + 1]
        blk = jnp.where(lane_o == 0, li_f, jnp.where(lane_o == 1, b_f, jnp.where(lane_o == 2, li_b, b_b)))
        col_ref[h] = blk
        row_ref[pl.ds(h * SUBLANES, SUBLANES), :] = blk.T[0:SUBLANES, :]


def _gate_prep(st, gates, b_gates, heads, chunk):
    n, tr = st.n, st.tm
    assert tr % chunk == 0
    bias = jnp.zeros((1, LANES), _F32).at[0, :4 * heads].set(b_gates)
    return pl.pallas_call(
        functools.partial(_gate_prep_kernel, heads=heads, chunk=chunk),
        grid=(n // tr,),
        in_specs=[
            pl.BlockSpec((tr, LANES), lambda i: (i, 0)),
            pl.BlockSpec((1, LANES), lambda i: (0, 0)),
        ],
        out_specs=[
            pl.BlockSpec((heads, tr, LANES), lambda i: (0, i, 0)),
            pl.BlockSpec((heads * SUBLANES, tr), lambda i: (0, i)),
        ],
        out_shape=[
            jax.ShapeDtypeStruct((heads, n, LANES), _F32),
            jax.ShapeDtypeStruct((heads * SUBLANES, n), _F32),
        ],
        compiler_params=_params("parallel"),
        name="mlstm_gate_prep",
    )(gates, bias)


def _scan_kernel(qx_ref, kx_ref, vx_ref, ox_ref, gcx_ref, grx_ref,
                 qc_ref, kc_ref, vc_ref, oc_ref, gcc_ref, grc_ref, hg_ref,
                 zx_ref, zc_ref,
                 cf_ref, mf_ref, cb_ref, mb_ref, vax_ref, vac_ref, hfx_ref, hbx_ref, hfc_ref, hbc_ref, *, chunk):
    T = chunk
    dv = vx_ref.shape[1]
    dqk = qx_ref.shape[1]
    row_i = lax.broadcasted_iota(jnp.int32, (T, T), 0)
    col_i = lax.broadcasted_iota(jnp.int32, (T, T), 1)

    for ref in (cf_ref, mf_ref, cb_ref, mb_ref):
        ref[...] = jnp.zeros_like(ref)
    for v_ref, va_ref in ((vx_ref, vax_ref), (vc_ref, vac_ref)):
        va_ref[:, :dv] = v_ref[...]
        va_ref[:, dv:] = jnp.ones((va_ref.shape[0], LANES), _BF)

    def step(q_ref, k_ref, va_ref, gc_ref, gr_ref, r0, c_ref, m_ref, backward):
        rows = slice(r0, r0 + T)
        q = q_ref[rows, :]
        k = k_ref[rows, :]
        va = va_ref[rows, :]
        gcb = gc_ref[rows, :]
        grb = gr_ref[:, rows]
        o = 2 if backward else 0
        li_rep = jnp.broadcast_to(gcb[:, o:o + 1], (T, LANES))
        b_rep = jnp.broadcast_to(gcb[:, o + 1:o + 2], (T, LANES))
        li_r, b_r = grb[o:o + 1, :], grb[o + 1:o + 2, :]
        m = m_ref[0:1, :]
        a_rep = b_rep + m
        mt_rep = jnp.maximum(a_rep, jnp.max(li_r, axis=1, keepdims=True))
        w_inter = jnp.exp(a_rep - mt_rep)
        d = _lane_tile(b_rep - mt_rep, T) + (li_r - b_r)
        p = jnp.exp(jnp.where((row_i <= col_i) if backward else (row_i >= col_i), d, -jnp.inf))
        qk = lax.dot_general(q, k, (((1,), (1,)), ((), ())), preferred_element_type=_F32)
        s = (qk * p).astype(_BF)
        c_old = c_ref[...]
        num = _lane_tile(w_inter, dv + LANES) * _dot(q, c_old.astype(_BF)) + _dot(s, va)
        den = num[:, dv:]
        inv = 1.0 / jnp.maximum(jnp.abs(den), jnp.exp(-mt_rep))
        h = num[:, :dv] * _lane_tile(inv, dv)
        b_tot = b_rep[0:1, :] if backward else b_rep[T - 1:T, :]
        gl = b_tot - b_rep + li_rep
        m_new = jnp.maximum(b_tot + m, jnp.max(gl, axis=0, keepdims=True))
        decay = jnp.exp(b_tot + m - m_new)
        kw = (k.astype(_F32) * _lane_tile(jnp.exp(gl - m_new), dqk)).astype(_BF)
        c_ref[...] = _lane_tile(decay, dv + LANES) * c_old + lax.dot_general(
            kw, va, (((0,), (0,)), ((), ())), preferred_element_type=_F32)
        m_ref[...] = jnp.broadcast_to(m_new, m_ref.shape)
        return h

    def run(q_ref, k_ref, va_ref, o_ref, gc_ref, gr_ref, hf_ref, hb_ref, z_ref):
        n_chunks = q_ref.shape[0] // T

        def finalize(ci):
            rows = slice(ci * T, (ci + 1) * T)
            hs = hf_ref[rows, :] + hb_ref[rows, :]
            hn = hs * lax.rsqrt(jnp.mean(hs * hs, axis=-1, keepdims=True) + EPS)
            gate = _sigmoid(o_ref[rows, :].astype(_F32))
            z_ref[rows, :] = (hn * hg_ref[...] * gate).astype(_BF)

        for i in range(n_chunks):
            jf, jb = i, n_chunks - 1 - i
            hf_ref[jf * T:(jf + 1) * T, :] = step(q_ref, k_ref, va_ref, gc_ref, gr_ref, jf * T, cf_ref, mf_ref, False)
            hb_ref[jb * T:(jb + 1) * T, :] = step(q_ref, k_ref, va_ref, gc_ref, gr_ref, jb * T, cb_ref, mb_ref, True)
            if jf >= jb:
                for ci in sorted({jf, jb}):
                    finalize(ci)

    run(qc_ref, kc_ref, vac_ref, oc_ref, gcc_ref, grc_ref, hfc_ref, hbc_ref, zc_ref)
    run(qx_ref, kx_ref, vax_ref, ox_ref, gcx_ref, grx_ref, hfx_ref, hbx_ref, zx_ref)


def _scan(batch, seq, ctx_len, d, qkvo_x, gcol_x, grow_x, qkvo_c, gcol_c, grow_c, hn_g, heads, chunk):
    dqk, dv = qkvo_x[0].shape[2], qkvo_x[2].shape[2]
    assert seq % chunk == 0 and ctx_len % chunk == 0 and chunk % LANES == 0 and dv % LANES == 0

    def stream_specs(length):
        def blk(width):
            return pl.BlockSpec((None, length, width), lambda b, h: (h, b, 0))
        return [blk(dqk), blk(dqk), blk(dv), blk(dv), blk(LANES),
                pl.BlockSpec((SUBLANES, length), lambda b, h: (h, b))]

    return pl.pallas_call(
        functools.partial(_scan_kernel, chunk=chunk),
        grid=(batch, heads),
        in_specs=stream_specs(seq) + stream_specs(ctx_len) + [pl.BlockSpec((1, dv), lambda b, h: (0, h))],
        out_specs=[
            pl.BlockSpec((seq, dv), lambda b, h: (b, h)),
            pl.BlockSpec((ctx_len, dv), lambda b, h: (b, h)),
        ],
        out_shape=[
            jax.ShapeDtypeStruct((batch * seq, d), _BF),
            jax.ShapeDtypeStruct((batch * ctx_len, d), _BF),
        ],
        scratch_shapes=[
            pltpu.VMEM((dqk, dv + LANES), _F32), pltpu.VMEM((SUBLANES, LANES), _F32),
            pltpu.VMEM((dqk, dv + LANES), _F32), pltpu.VMEM((SUBLANES, LANES), _F32),
            pltpu.VMEM((seq, dv + LANES), _BF), pltpu.VMEM((ctx_len, dv + LANES), _BF),
            pltpu.VMEM((seq, dv), _F32), pltpu.VMEM((seq, dv), _F32),
            pltpu.VMEM((ctx_len, dv), _F32), pltpu.VMEM((ctx_len, dv), _F32),
        ],
        compiler_params=_params("parallel", "parallel"),
        name="mlstm_scan",
    )(*qkvo_x, gcol_x, grow_x, *qkvo_c, gcol_c, grow_c, hn_g.reshape(1, d))


def kernel(x, c, ctx, c_ctx, norm1_g, norm2_g, w_mod, b_mod, w_gu, w_down, conv_w_pw1, conv_b_pw1,
           conv_w_dw, conv_b_dw, conv_ln_g, conv_ln_b, conv_w_pw2, conv_b_pw2, m_w_in, m_b_gates,
           m_hn_g, m_w_out, final_g):
    batch, seq, d = x.shape
    ctx_len = ctx.shape[1]
    depth = w_mod.shape[0]
    heads = max(4, d // 512)
    dqk_all = d // 2
    chunk = _tile(ctx_len, 256)
    assert batch + 1 <= MOD_ROWS and 4 * heads <= LANES

    sx = _Stream(batch * seq, d, None)
    assert seq % sx.tm == 0
    tiles_per_sample = seq // sx.tm
    sx.mod_row = lambda i: i // tiles_per_sample
    sc = _Stream(batch * ctx_len, d, lambda i: batch)

    cc = jnp.zeros((MOD_ROWS, d), _F32).at[:batch].set(c).at[batch].set(c_ctx)
    mods = _modulation(cc, w_mod, b_mod).reshape(depth, MOD_ROWS, 6, 1, d)

    ff = w_down.shape[1]
    tf = _tile(ff, COL_TILE)
    w_gu_b = w_gu.astype(_BF).reshape(depth, d, 2, ff // tf, tf).transpose(0, 2, 3, 1, 4)
    w_down_b = w_down.astype(_BF)
    w_pw1_b, w_pw2_b = conv_w_pw1.astype(_BF), conv_w_pw2.astype(_BF)
    n_main = 2 * dqk_all + 2 * d
    w_in_b = m_w_in.astype(_BF)
    w_gates_b = jnp.zeros((m_w_in.shape[0], d, LANES), _BF).at[:, :, :4 * heads].set(
        m_w_in[:, :, n_main:].astype(_BF))
    w_out_b = m_w_out.astype(_BF)

    xs = x.reshape(batch * seq, d)
    cs = ctx.reshape(batch * ctx_len, d)
    for i in range(depth):
        last = i == depth - 1
        j = i // 2
        ffn_w = (norm2_g[i], _Layered(w_gu_b, i), _Layered(w_down_b, i), final_g)
        if i % 2 == 0:
            conv_w = (conv_ln_g[j], conv_ln_b[j], _Layered(w_pw2_b, j), conv_b_pw2[j])
            w_pw1 = _Layered(w_pw1_b, j)
            ux = _pw1_glu(sx, xs, mods, i, norm1_g[i], w_pw1, conv_b_pw1[j])
            vx = _dwconv(ux, batch, seq, True, conv_w_dw[j], conv_b_dw[j])
            if not last:
                uc = _pw1_glu(sc, cs, mods, i, norm1_g[i], w_pw1, conv_b_pw1[j])
                vc = _dwconv(uc, batch, ctx_len, False, conv_w_dw[j], conv_b_dw[j])
                cs = _conv_out_ffn(sc, vc, cs, mods, i, *conv_w, *ffn_w, False)
            xs = _conv_out_ffn(sx, vx, xs, mods, i, *conv_w, *ffn_w, last)
        else:
            in_w = (_Layered(w_in_b, j), heads, dqk_all // heads, d // heads, _Layered(w_gates_b, j))
            w_out = _Layered(w_out_b, j)
            *px, gx = _min_proj(sx, xs, mods, i, norm1_g[i], *in_w)
            *pc, gc = _min_proj(sc, cs, mods, i, norm1_g[i], *in_w)
            gcol_x, grow_x = _gate_prep(sx, gx, m_b_gates[j], heads, chunk)
            gcol_c, grow_c = _gate_prep(sc, gc, m_b_gates[j], heads, chunk)
            zx, zc = _scan(batch, seq, ctx_len, d, px, gcol_x, grow_x, pc, gcol_c, grow_c,
                           m_hn_g[j], heads, chunk)
            if not last:
                cs = _mlstm_out_ffn(sc, zc, cs, mods, i, w_out, *ffn_w, False)
            xs = _mlstm_out_ffn(sx, zx, xs, mods, i, w_out, *ffn_w, last)
    return xs.reshape(batch, seq, d)
```

```python
import functools

import jax
import jax.numpy as jnp
from jax import lax
from jax.experimental import pallas as pl
from jax.experimental.pallas import tpu as pltpu

EPS = 1e-6
GATE_CAP = 15.0
GRID_W = 64
LANES = 128
SUBLANES = 8
VMEM_LIMIT = 56 * 1024 * 1024
MOD_ROWS = 32
ROW_TILE = 512
COL_TILE = 512

_BF = jnp.bfloat16
_F32 = jnp.float32


def _tile(dim, pref):
    t = min(dim, pref)
    while dim % t:
        t //= 2
    return t


def _params(*sem):
    return pltpu.CompilerParams(dimension_semantics=sem, vmem_limit_bytes=VMEM_LIMIT)


def _sigmoid(v):
    return 0.5 * (1.0 + jnp.tanh(0.5 * v))


def _silu(v):
    h = 0.5 * v
    return h + h * jnp.tanh(h)


def _lane_tile(rep, width):
    if width % LANES == 0:
        return rep if width == LANES else jnp.concatenate([rep] * (width // LANES), axis=1)
    assert width < LANES
    return rep[:, :width]


def _dot(a, b):
    return jnp.dot(a, b, preferred_element_type=_F32)


def _rms_mod(x, g, shift, scale):
    y = x * lax.rsqrt(jnp.mean(x * x, axis=-1, keepdims=True) + EPS)
    return y * (g * (1.0 + scale)) + shift


def _resident(shape):
    zeros = (0,) * len(shape)
    return pl.BlockSpec(shape, lambda *_: zeros, pipeline_mode=pl.Buffered(1))


class _Layered:
    def __init__(self, stacked, index):
        self.arr, self.index = stacked, index
        self.shape = stacked.shape[1:]

    def resident(self, shape=None):
        shape = self.shape if shape is None else shape
        index = (self.index,) + (0,) * len(shape)
        return pl.BlockSpec((None,) + tuple(shape), lambda *_: index, pipeline_mode=pl.Buffered(1))

    def tiled(self, block, index_map):
        return pl.BlockSpec((None,) + tuple(block), lambda *a: (self.index,) + tuple(index_map(*a)))


def _mod_kernel(cc_ref, w_ref, b_ref, o_ref):
    cc = cc_ref[...]
    s = _silu(cc).astype(_BF)
    o_ref[...] = _dot(s, w_ref[...].astype(_BF)) + b_ref[...]


def _modulation(cc, w_mod, b_mod):
    depth, d, n = w_mod.shape
    tn = _tile(n, 1024)
    return pl.pallas_call(
        _mod_kernel,
        grid=(depth, n // tn),
        in_specs=[
            pl.BlockSpec((MOD_ROWS, d), lambda l, j: (0, 0)),
            pl.BlockSpec((None, d, tn), lambda l, j: (l, 0, j)),
            pl.BlockSpec((None, 1, tn), lambda l, j: (l, 0, j)),
        ],
        out_specs=pl.BlockSpec((None, MOD_ROWS, tn), lambda l, j: (l, 0, j)),
        out_shape=jax.ShapeDtypeStruct((depth, MOD_ROWS, n), _F32),
        compiler_params=_params("parallel", "parallel"),
        name="modulation",
    )(cc, w_mod, b_mod.reshape(depth, 1, n))


class _Stream:
    def __init__(self, n_rows, d, mod_row):
        self.n, self.d = n_rows, d
        self.tm = _tile(n_rows, ROW_TILE)
        self.tiles = n_rows // self.tm
        self.mod_row = mod_row

    def mod_spec(self, layer, part):
        return pl.BlockSpec((None, None, None, 1, self.d),
                            lambda i, *_: (layer, self.mod_row(i), part, 0, 0))


def _pw1_glu_kernel(x_ref, g_ref, sh_ref, sc_ref, w_ref, b_ref, u_ref, h_ref, *, tn):
    h_ref[...] = _rms_mod(x_ref[...], g_ref[...], sh_ref[...], sc_ref[...]).astype(_BF)
    d = h_ref.shape[1]
    for j in range(d // tn):
        lo, hi = j * tn, (j + 1) * tn
        a = _dot(h_ref[...], w_ref[:, lo:hi]) + b_ref[:, lo:hi]
        b = _dot(h_ref[...], w_ref[:, d + lo:d + hi]) + b_ref[:, d + lo:d + hi]
        u_ref[:, lo:hi] = (a * _sigmoid(b)).astype(u_ref.dtype)


def _pw1_glu(st, xs, mods, layer, g1, w_pw1, b_pw1):
    d, tm = st.d, st.tm
    return pl.pallas_call(
        functools.partial(_pw1_glu_kernel, tn=_tile(d, COL_TILE)),
        grid=(st.tiles,),
        in_specs=[
            pl.BlockSpec((tm, d), lambda i: (i, 0)),
            _resident((1, d)),
            st.mod_spec(layer, 0),
            st.mod_spec(layer, 1),
            w_pw1.resident(),
            _resident((1, 2 * d)),
        ],
        out_specs=pl.BlockSpec((tm, d), lambda i: (i, 0)),
        out_shape=jax.ShapeDtypeStruct((st.n, d), _BF),
        scratch_shapes=[pltpu.VMEM((tm, d), _BF)],
        compiler_params=_params("parallel"),
        name="pw1_glu",
    )(xs, g1.reshape(1, d), mods, mods, w_pw1.arr, b_pw1.reshape(1, 2 * d))


_CONV_LPAD = 16


def _conv_along_sublanes(pad_ref, w_ref, b_ref, n_rows, row_len, taps, store):
    tc = pad_ref.shape[-1]
    first = _CONV_LPAD - taps // 2

    def body(r, carry):
        for l0 in range(0, tc, LANES):
            lanes = pl.ds(l0, LANES)
            acc = jnp.zeros((row_len, LANES), _F32)
            for s in range(SUBLANES):
                ks = [k for k in range(taps) if (first + k) % SUBLANES == s]
                if not ks:
                    continue
                lo, hi = first + ks[0], first + ks[-1]
                n = row_len + hi - lo + (SUBLANES if s else 0)
                span = pad_ref[r, pl.ds(lo - s, n), lanes]
                if s:
                    span = pltpu.roll(span, n - s, 0)
                for k in ks:
                    off = first + k - lo
                    acc = acc + span[off:off + row_len] * w_ref[pl.ds(k, 1), lanes]
            store(r, l0, acc + b_ref[:, lanes])
        return carry
    lax.fori_loop(0, n_rows, body, 0)


def _dwconv_grid_kernel(u_ref, w_ref, b_ref, o_ref, padh_ref, padv_ref, *, n_h_tiles, taps):
    rows, _, tc = u_ref.shape
    half = taps // 2

    def store(r, l0, val):
        o_ref[r, :, pl.ds(l0, LANES)] = val.astype(o_ref.dtype)

    @pl.when(pl.program_id(1) < n_h_tiles)
    def _():
        zeros = jnp.zeros((rows, _CONV_LPAD, tc), _F32)
        padh_ref[:, pl.ds(0, _CONV_LPAD), :] = zeros
        padh_ref[:, pl.ds(_CONV_LPAD + GRID_W, _CONV_LPAD), :] = zeros
        padh_ref[:, pl.ds(_CONV_LPAD, GRID_W), :] = u_ref[...].astype(_F32)
        _conv_along_sublanes(padh_ref, w_ref, b_ref, rows, GRID_W, taps, store)

    @pl.when(pl.program_id(1) >= n_h_tiles)
    def _():
        zeros = jnp.zeros((half, GRID_W, tc), _F32)
        padv_ref[pl.ds(0, half)] = zeros
        padv_ref[pl.ds(half + rows, half)] = zeros
        padv_ref[pl.ds(half, rows)] = u_ref[...].astype(_F32)

        def body(r, carry):
            for l0 in range(0, tc, LANES):
                lanes = pl.ds(l0, LANES)
                acc = jnp.zeros((GRID_W, LANES), _F32)
                for k in range(taps):
                    acc = acc + padv_ref[r + k, :, lanes] * w_ref[pl.ds(k, 1), lanes]
                store(r, l0, acc + b_ref[:, lanes])
            return carry
        lax.fori_loop(0, rows, body, 0)


def _dwconv_seq_kernel(u_ref, w_ref, b_ref, o_ref, pad_ref, *, taps):
    n_seq, seq_len, tc = u_ref.shape
    zeros = jnp.zeros((n_seq, _CONV_LPAD, tc), _F32)
    pad_ref[:, pl.ds(0, _CONV_LPAD), :] = zeros
    pad_ref[:, pl.ds(_CONV_LPAD + seq_len, _CONV_LPAD), :] = zeros
    pad_ref[:, pl.ds(_CONV_LPAD, seq_len), :] = u_ref[...].astype(_F32)

    def store(r, l0, val):
        o_ref[r, :, pl.ds(l0, LANES)] = val.astype(o_ref.dtype)
    _conv_along_sublanes(pad_ref, w_ref, b_ref, n_seq, seq_len, taps, store)


def _dwconv(u, batch, seq, on_grid, w_dw, b_dw):
    n, d = u.shape
    taps = w_dw.shape[0]
    assert taps // 2 <= _CONV_LPAD
    tc = _tile(d // 2, 256)
    wspecs = [pl.BlockSpec((taps, tc), lambda i, c: (0, c)), pl.BlockSpec((1, tc), lambda i, c: (0, c))]
    if on_grid:
        assert seq % GRID_W == 0
        rows = seq // GRID_W
        block = pl.BlockSpec((rows, GRID_W, tc), lambda i, c: (i, 0, c))
        out = pl.pallas_call(
            functools.partial(_dwconv_grid_kernel, n_h_tiles=(d // 2) // tc, taps=taps),
            grid=(batch, d // tc),
            in_specs=[block] + wspecs,
            out_specs=block,
            out_shape=jax.ShapeDtypeStruct((n // GRID_W, GRID_W, d), u.dtype),
            scratch_shapes=[
                pltpu.VMEM((rows, GRID_W + 2 * _CONV_LPAD, tc), _F32),
                pltpu.VMEM((rows + 2 * (taps // 2), GRID_W, tc), _F32),
            ],
            compiler_params=_params("parallel", "parallel"),
            name="dwconv_grid",
        )(u.reshape(n // GRID_W, GRID_W, d), w_dw, b_dw.reshape(1, d))
    else:
        bb = _tile(batch, 8)
        block = pl.BlockSpec((bb, seq, tc), lambda i, c: (i, 0, c))
        out = pl.pallas_call(
            functools.partial(_dwconv_seq_kernel, taps=taps),
            grid=(batch // bb, d // tc),
            in_specs=[block] + wspecs,
            out_specs=block,
            out_shape=jax.ShapeDtypeStruct((batch, seq, d), u.dtype),
            scratch_shapes=[pltpu.VMEM((bb, seq + 2 * _CONV_LPAD, tc), _F32)],
            compiler_params=_params("parallel", "parallel"),
            name="dwconv_seq",
        )(u.reshape(batch, seq, d), w_dw, b_dw.reshape(1, d))
    return out.reshape(n, d)


def _ffn_step(h_ref, gate2_ref, wg_ref, wu_ref, wd_ref, fg_ref, o_ref, final_norm):
    h = h_ref[...]
    g = _dot(h, wg_ref[...])
    u = _dot(h, wu_ref[...])
    act = (_silu(g) * u).astype(_BF)
    o_ref[...] += gate2_ref[...] * _dot(act, wd_ref[...])
    if final_norm:
        @pl.when(pl.program_id(1) == pl.num_programs(1) - 1)
        def _():
            y = o_ref[...]
            o_ref[...] = y * lax.rsqrt(jnp.mean(y * y, axis=-1, keepdims=True) + EPS) * fg_ref[...]


def _conv_out_ffn_kernel(v_ref, lg_ref, lb_ref, w2_ref, b2_ref, x_ref, gate1_ref, g_ref, sh_ref, sc_ref,
                         gate2_ref, wg_ref, wu_ref, wd_ref, fg_ref, o_ref, h_ref, *, final_norm):
    @pl.when(pl.program_id(1) == 0)
    def _():
        v = v_ref[...].astype(_F32)
        mu = jnp.mean(v, axis=-1, keepdims=True)
        vc = v - mu
        var = jnp.mean(vc * vc, axis=-1, keepdims=True)
        y = vc * lax.rsqrt(var + EPS) * lg_ref[...] + lb_ref[...]
        z = _silu(y).astype(_BF)
        x1 = x_ref[...] + gate1_ref[...] * (_dot(z, w2_ref[...]) + b2_ref[...])
        o_ref[...] = x1
        h_ref[...] = _rms_mod(x1, g_ref[...], sh_ref[...], sc_ref[...]).astype(_BF)

    _ffn_step(h_ref, gate2_ref, wg_ref, wu_ref, wd_ref, fg_ref, o_ref, final_norm)


def _mlstm_out_ffn_kernel(z_ref, w2_ref, x_ref, gate1_ref, g_ref, sh_ref, sc_ref,
                          gate2_ref, wg_ref, wu_ref, wd_ref, fg_ref, o_ref, h_ref, *, final_norm):
    @pl.when(pl.program_id(1) == 0)
    def _():
        x1 = x_ref[...] + gate1_ref[...] * _dot(z_ref[...], w2_ref[...])
        o_ref[...] = x1
        h_ref[...] = _rms_mod(x1, g_ref[...], sh_ref[...], sc_ref[...]).astype(_BF)

    _ffn_step(h_ref, gate2_ref, wg_ref, wu_ref, wd_ref, fg_ref, o_ref, final_norm)


def _mix_out_ffn(st, mix_args, mix_specs, kern, xs, mods, layer, g2, w_gu, w_down, final_g, final_norm):
    d, tm = st.d, st.tm
    ff = w_down.shape[0]
    tf = _tile(ff, COL_TILE)
    nf = ff // tf
    row_block = pl.BlockSpec((tm, d), lambda i, f: (i, 0))
    return pl.pallas_call(
        functools.partial(kern, final_norm=final_norm),
        grid=(st.tiles, nf),
        in_specs=mix_specs + [
            row_block,
            st.mod_spec(layer, 2),
            _resident((1, d)),
            st.mod_spec(layer, 3),
            st.mod_spec(layer, 4),
            st.mod_spec(layer, 5),
            w_gu.tiled((d, tf), lambda i, f: (0, f)),
            w_gu.tiled((d, tf), lambda i, f: (0, nf + f)),
            w_down.tiled((tf, d), lambda i, f: (f, 0)),
            _resident((1, d)),
        ],
        out_specs=row_block,
        out_shape=jax.ShapeDtypeStruct((st.n, d), _F32),
        scratch_shapes=[pltpu.VMEM((tm, d), _BF)],
        compiler_params=_params("parallel", "arbitrary"),
        name=kern.__name__.strip("_").replace("_kernel", ""),
    )(*mix_args, xs, mods, g2.reshape(1, d), mods, mods, mods, w_gu.arr, w_gu.arr, w_down.arr,
      final_g.reshape(1, d))


def _conv_out_ffn(st, v, xs, mods, layer, ln_g, ln_b, w_pw2, b_pw2, g2, w_gu, w_down, final_g, final_norm):
    d = st.d
    specs = [pl.BlockSpec((st.tm, d), lambda i, f: (i, 0)), _resident((1, d)), _resident((1, d)),
             w_pw2.resident(), _resident((1, d))]
    args = (v, ln_g.reshape(1, d), ln_b.reshape(1, d), w_pw2.arr, b_pw2.reshape(1, d))
    return _mix_out_ffn(st, args, specs, _conv_out_ffn_kernel, xs, mods, layer, g2, w_gu, w_down,
                        final_g, final_norm)


def _mlstm_out_ffn(st, z, xs, mods, layer, w_out, g2, w_gu, w_down, final_g, final_norm):
    d = st.d
    specs = [pl.BlockSpec((st.tm, d), lambda i, f: (i, 0)), w_out.resident()]
    return _mix_out_ffn(st, (z, w_out.arr), specs, _mlstm_out_ffn_kernel, xs, mods, layer, g2, w_gu, w_down,
                        final_g, final_norm)


def _log_sigmoid(v):
    return jnp.minimum(v, 0.0) - jnp.log(1.0 + jnp.exp(-jnp.abs(v)))


def _prepare_gates(g, col_ref, row_ref, heads, chunk):
    rows = g.shape[0]
    g = GATE_CAP * jnp.tanh(g / GATE_CAP)
    lane = lax.broadcasted_iota(jnp.int32, g.shape, 1)
    is_f = (lane // heads) % 2 == 1
    vals = jnp.where(is_f, _log_sigmoid(g), g)
    t = lax.broadcasted_iota(jnp.int32, g.shape, 0) % chunk
    pre = vals
    suf = vals
    shift = 1
    while shift < chunk:
        pre = pre + jnp.where(t >= shift, pltpu.roll(pre, shift, 0), 0.0)
        suf = suf + jnp.where(t + shift < chunk, pltpu.roll(suf, rows - shift, 0), 0.0)
        shift *= 2
    lane_o = lax.broadcasted_iota(jnp.int32, (rows, LANES), 1)
    for h in range(heads):
        li_f = vals[:, h:h + 1]
        b_f = pre[:, heads + h:heads + h + 1]
        li_b = vals[:, 2 * heads + h:2 * heads + h + 1]
        b_b = suf[:, 3 * heads + h:3 * heads + h + 1]
        blk = jnp.where(lane_o == 0, li_f, jnp.where(lane_o == 1, b_f, jnp.where(lane_o == 2, li_b, b_b)))
        col_ref[h] = blk
        row_ref[pl.ds(h * SUBLANES, SUBLANES), :] = blk.T[0:SUBLANES, :]


def _min_proj_kernel(x_ref, g_ref, sh_ref, sc_ref, w_ref, wg_ref, bg_ref, q_ref, k_ref, v_ref, o_ref,
                     col_ref, row_ref, h_ref, *, tn, chunk):
    h_ref[...] = _rms_mod(x_ref[...], g_ref[...], sh_ref[...], sc_ref[...]).astype(_BF)
    heads, _, dqk = q_ref.shape
    dv = v_ref.shape[2]
    _prepare_gates(_dot(h_ref[...], wg_ref[...]) + bg_ref[...], col_ref, row_ref, heads, chunk)
    k_scale = float(dqk) ** -0.5
    pieces, start = [], 0
    for ref, width, scale in ((q_ref, dqk, None), (k_ref, dqk, k_scale), (v_ref, dv, None), (o_ref, dv, None)):
        for h in range(heads):
            pieces.append((start, width, ref, h, scale))
            start += width
    for lo in range(0, start, tn):
        hi = lo + tn
        acc = _dot(h_ref[...], w_ref[:, lo:hi])
        for ps, width, ref, h, scale in pieces:
            a, b = max(lo, ps), min(hi, ps + width)
            if a < b:
                part = acc[:, a - lo:b - lo]
                ref[h, :, a - ps:b - ps] = (part if scale is None else part * scale).astype(_BF)


def _min_proj(st, xs, mods, layer, g1, w_in, heads, dqk, dv, w_gates, b_gates, chunk):
    d, tm = st.d, st.tm
    n_main = heads * (2 * dqk + 2 * dv)
    assert tm % chunk == 0 and 4 * heads <= LANES
    bias = jnp.zeros((1, LANES), _F32).at[0, :4 * heads].set(b_gates)

    def head_major(width, dtype):
        return (pl.BlockSpec((heads, tm, width), lambda i: (0, i, 0)),
                jax.ShapeDtypeStruct((heads, st.n, width), dtype))
    outs = [head_major(dqk, _BF), head_major(dqk, _BF), head_major(dv, _BF), head_major(dv, _BF),
            head_major(LANES, _F32),
            (pl.BlockSpec((heads * SUBLANES, tm), lambda i: (0, i)),
             jax.ShapeDtypeStruct((heads * SUBLANES, st.n), _F32))]
    return pl.pallas_call(
        functools.partial(_min_proj_kernel, tn=_tile(n_main, COL_TILE), chunk=chunk),
        grid=(st.tiles,),
        in_specs=[
            pl.BlockSpec((tm, d), lambda i: (i, 0)),
            _resident((1, d)),
            st.mod_spec(layer, 0),
            st.mod_spec(layer, 1),
            w_in.resident((d, n_main)),
            w_gates.resident(),
            _resident((1, LANES)),
        ],
        out_specs=[o[0] for o in outs],
        out_shape=[o[1] for o in outs],
        scratch_shapes=[pltpu.VMEM((tm, d), _BF)],
        compiler_params=_params("parallel"),
        name="mlstm_in_proj",
    )(xs, g1.reshape(1, d), mods, mods, w_in.arr, w_gates.arr, bias)


def _scan_kernel(qx_ref, kx_ref, vx_ref, ox_ref, gcx_ref, grx_ref,
                 qc_ref, kc_ref, vc_ref, oc_ref, gcc_ref, grc_ref, hg_ref,
                 zx_ref, zc_ref,
                 cf_ref, mf_ref, cb_ref, mb_ref, vax_ref, vac_ref, hfx_ref, hbx_ref, hfc_ref, hbc_ref, *, chunk):
    T = chunk
    dv = vx_ref.shape[1]
    dqk = qx_ref.shape[1]
    row_i = lax.broadcasted_iota(jnp.int32, (T, T), 0)
    col_i = lax.broadcasted_iota(jnp.int32, (T, T), 1)

    for ref in (cf_ref, mf_ref, cb_ref, mb_ref):
        ref[...] = jnp.zeros_like(ref)
    for v_ref, va_ref in ((vx_ref, vax_ref), (vc_ref, vac_ref)):
        va_ref[:, :dv] = v_ref[...]
        va_ref[:, dv:] = jnp.ones((va_ref.shape[0], LANES), _BF)

    def step(q_ref, k_ref, va_ref, gc_ref, gr_ref, r0, c_ref, m_ref, backward):
        rows = slice(r0, r0 + T)
        q = q_ref[rows, :]
        k = k_ref[rows, :]
        va = va_ref[rows, :]
        gcb = gc_ref[rows, :]
        grb = gr_ref[:, rows]
        o = 2 if backward else 0
        li_rep = jnp.broadcast_to(gcb[:, o:o + 1], (T, LANES))
        b_rep = jnp.broadcast_to(gcb[:, o + 1:o + 2], (T, LANES))
        li_r, b_r = grb[o:o + 1, :], grb[o + 1:o + 2, :]
        m = m_ref[0:1, :]
        a_rep = b_rep + m
        mt_rep = jnp.maximum(a_rep, jnp.max(li_r, axis=1, keepdims=True))
        w_inter = jnp.exp(a_rep - mt_rep)
        d = _lane_tile(b_rep - mt_rep, T) + (li_r - b_r)
        p = jnp.exp(jnp.where((row_i <= col_i) if backward else (row_i >= col_i), d, -jnp.inf))
        qk = lax.dot_general(q, k, (((1,), (1,)), ((), ())), preferred_element_type=_F32)
        s = (qk * p).astype(_BF)
        c_old = c_ref[...]
        num = _lane_tile(w_inter, dv + LANES) * _dot(q, c_old.astype(_BF)) + _dot(s, va)
        den = num[:, dv:]
        inv = 1.0 / jnp.maximum(jnp.abs(den), jnp.exp(-mt_rep))
        h = num[:, :dv] * _lane_tile(inv, dv)
        b_tot = b_rep[0:1, :] if backward else b_rep[T - 1:T, :]
        gl = b_tot - b_rep + li_rep
        m_new = jnp.maximum(b_tot + m, jnp.max(gl, axis=0, keepdims=True))
        decay = jnp.exp(b_tot + m - m_new)
        kw = (k.astype(_F32) * _lane_tile(jnp.exp(gl - m_new), dqk)).astype(_BF)
        c_ref[...] = _lane_tile(decay, dv + LANES) * c_old + lax.dot_general(
            kw, va, (((0,), (0,)), ((), ())), preferred_element_type=_F32)
        m_ref[...] = jnp.broadcast_to(m_new, m_ref.shape)
        return h

    def run(q_ref, k_ref, va_ref, o_ref, gc_ref, gr_ref, hf_ref, hb_ref, z_ref):
        n_chunks = q_ref.shape[0] // T

        def finalize(ci):
            rows = slice(ci * T, (ci + 1) * T)
            hs = hf_ref[rows, :] + hb_ref[rows, :]
            hn = hs * lax.rsqrt(jnp.mean(hs * hs, axis=-1, keepdims=True) + EPS)
            gate = _sigmoid(o_ref[rows, :].astype(_F32))
            z_ref[rows, :] = (hn * hg_ref[...] * gate).astype(_BF)

        for i in range(n_chunks):
            jf, jb = i, n_chunks - 1 - i
            hf_ref[jf * T:(jf + 1) * T, :] = step(q_ref, k_ref, va_ref, gc_ref, gr_ref, jf * T, cf_ref, mf_ref, False)
            hb_ref[jb * T:(jb + 1) * T, :] = step(q_ref, k_ref, va_ref, gc_ref, gr_ref, jb * T, cb_ref, mb_ref, True)
            if jf >= jb:
                for ci in sorted({jf, jb}):
                    finalize(ci)

    run(qc_ref, kc_ref, vac_ref, oc_ref, gcc_ref, grc_ref, hfc_ref, hbc_ref, zc_ref)
    run(qx_ref, kx_ref, vax_ref, ox_ref, gcx_ref, grx_ref, hfx_ref, hbx_ref, zx_ref)


def _scan(batch, seq, ctx_len, d, qkvo_x, gcol_x, grow_x, qkvo_c, gcol_c, grow_c, hn_g, heads, chunk):
    dqk, dv = qkvo_x[0].shape[2], qkvo_x[2].shape[2]
    assert seq % chunk == 0 and ctx_len % chunk == 0 and chunk % LANES == 0 and dv % LANES == 0

    def stream_specs(length):
        def blk(width):
            return pl.BlockSpec((None, length, width), lambda b, h: (h, b, 0))
        return [blk(dqk), blk(dqk), blk(dv), blk(dv), blk(LANES),
                pl.BlockSpec((SUBLANES, length), lambda b, h: (h, b))]

    return pl.pallas_call(
        functools.partial(_scan_kernel, chunk=chunk),
        grid=(batch, heads),
        in_specs=stream_specs(seq) + stream_specs(ctx_len) + [pl.BlockSpec((1, dv), lambda b, h: (0, h))],
        out_specs=[
            pl.BlockSpec((seq, dv), lambda b, h: (b, h)),
            pl.BlockSpec((ctx_len, dv), lambda b, h: (b, h)),
        ],
        out_shape=[
            jax.ShapeDtypeStruct((batch * seq, d), _BF),
            jax.ShapeDtypeStruct((batch * ctx_len, d), _BF),
        ],
        scratch_shapes=[
            pltpu.VMEM((dqk, dv + LANES), _F32), pltpu.VMEM((SUBLANES, LANES), _F32),
            pltpu.VMEM((dqk, dv + LANES), _F32), pltpu.VMEM((SUBLANES, LANES), _F32),
            pltpu.VMEM((seq, dv + LANES), _BF), pltpu.VMEM((ctx_len, dv + LANES), _BF),
            pltpu.VMEM((seq, dv), _F32), pltpu.VMEM((seq, dv), _F32),
            pltpu.VMEM((ctx_len, dv), _F32), pltpu.VMEM((ctx_len, dv), _F32),
        ],
        compiler_params=_params("parallel", "parallel"),
        name="mlstm_scan",
    )(*qkvo_x, gcol_x, grow_x, *qkvo_c, gcol_c, grow_c, hn_g.reshape(1, d))


def kernel(x, c, ctx, c_ctx, norm1_g, norm2_g, w_mod, b_mod, w_gu, w_down, conv_w_pw1, conv_b_pw1,
           conv_w_dw, conv_b_dw, conv_ln_g, conv_ln_b, conv_w_pw2, conv_b_pw2, m_w_in, m_b_gates,
           m_hn_g, m_w_out, final_g):
    batch, seq, d = x.shape
    ctx_len = ctx.shape[1]
    depth = w_mod.shape[0]
    heads = max(4, d // 512)
    dqk_all = d // 2
    chunk = _tile(ctx_len, 256)
    assert batch + 1 <= MOD_ROWS and 4 * heads <= LANES

    sx = _Stream(batch * seq, d, None)
    assert seq % sx.tm == 0
    tiles_per_sample = seq // sx.tm
    sx.mod_row = lambda i: i // tiles_per_sample
    sc = _Stream(batch * ctx_len, d, lambda i: batch)

    cc = jnp.zeros((MOD_ROWS, d), _F32).at[:batch].set(c).at[batch].set(c_ctx)
    mods = _modulation(cc, w_mod, b_mod).reshape(depth, MOD_ROWS, 6, 1, d)

    w_gu_b, w_down_b = w_gu.astype(_BF), w_down.astype(_BF)
    w_pw1_b, w_pw2_b = conv_w_pw1.astype(_BF), conv_w_pw2.astype(_BF)
    n_main = 2 * dqk_all + 2 * d
    w_in_b = m_w_in.astype(_BF)
    w_gates_b = jnp.zeros((m_w_in.shape[0], d, LANES), _BF).at[:, :, :4 * heads].set(
        m_w_in[:, :, n_main:].astype(_BF))
    w_out_b = m_w_out.astype(_BF)

    xs = x.reshape(batch * seq, d)
    cs = ctx.reshape(batch * ctx_len, d)
    for i in range(depth):
        last = i == depth - 1
        j = i // 2
        ffn_w = (norm2_g[i], _Layered(w_gu_b, i), _Layered(w_down_b, i), final_g)
        if i % 2 == 0:
            conv_w = (conv_ln_g[j], conv_ln_b[j], _Layered(w_pw2_b, j), conv_b_pw2[j])
            w_pw1 = _Layered(w_pw1_b, j)
            ux = _pw1_glu(sx, xs, mods, i, norm1_g[i], w_pw1, conv_b_pw1[j])
            vx = _dwconv(ux, batch, seq, True, conv_w_dw[j], conv_b_dw[j])
            if not last:
                uc = _pw1_glu(sc, cs, mods, i, norm1_g[i], w_pw1, conv_b_pw1[j])
                vc = _dwconv(uc, batch, ctx_len, False, conv_w_dw[j], conv_b_dw[j])
                cs = _conv_out_ffn(sc, vc, cs, mods, i, *conv_w, *ffn_w, False)
            xs = _conv_out_ffn(sx, vx, xs, mods, i, *conv_w, *ffn_w, last)
        else:
            in_w = (_Layered(w_in_b, j), heads, dqk_all // heads, d // heads, _Layered(w_gates_b, j),
                    m_b_gates[j], chunk)
            w_out = _Layered(w_out_b, j)
            *px, gcol_x, grow_x = _min_proj(sx, xs, mods, i, norm1_g[i], *in_w)
            *pc, gcol_c, grow_c = _min_proj(sc, cs, mods, i, norm1_g[i], *in_w)
            zx, zc = _scan(batch, seq, ctx_len, d, px, gcol_x, grow_x, pc, gcol_c, grow_c,
                           m_hn_g[j], heads, chunk)
            if not last:
                cs = _mlstm_out_ffn(sc, zc, cs, mods, i, w_out, *ffn_w, False)
            xs = _mlstm_out_ffn(sx, zx, xs, mods, i, w_out, *ffn_w, last)
    return xs.reshape(batch, seq, d)
```

```python
import functools

import jax
import jax.numpy as jnp
from jax import lax
from jax.experimental import pallas as pl
from jax.experimental.pallas import tpu as pltpu

EPS = 1e-6
GATE_CAP = 15.0
GRID_W = 64
LANES = 128
SUBLANES = 8
VMEM_LIMIT = 56 * 1024 * 1024
MOD_ROWS = 32
ROW_TILE = 512
COL_TILE = 512

_BF = jnp.bfloat16
_F32 = jnp.float32


def _tile(dim, pref):
    t = min(dim, pref)
    while dim % t:
        t //= 2
    return t


def _params(*sem):
    return pltpu.CompilerParams(dimension_semantics=sem, vmem_limit_bytes=VMEM_LIMIT)


def _sigmoid(v):
    return 0.5 * (1.0 + jnp.tanh(0.5 * v))


def _silu(v):
    h = 0.5 * v
    return h + h * jnp.tanh(h)


def _lane_tile(rep, width):
    if width % LANES == 0:
        return rep if width == LANES else jnp.concatenate([rep] * (width // LANES), axis=1)
    assert width < LANES
    return rep[:, :width]


def _dot(a, b):
    return jnp.dot(a, b, preferred_element_type=_F32)


def _rms_mod(x, g, shift, scale):
    y = x * lax.rsqrt(jnp.mean(x * x, axis=-1, keepdims=True) + EPS)
    return y * (g * (1.0 + scale)) + shift


def _resident(shape):
    zeros = (0,) * len(shape)
    return pl.BlockSpec(shape, lambda *_: zeros, pipeline_mode=pl.Buffered(1))


class _Layered:
    def __init__(self, stacked, index):
        self.arr, self.index = stacked, index
        self.shape = stacked.shape[1:]

    def resident(self, shape=None):
        shape = self.shape if shape is None else shape
        index = (self.index,) + (0,) * len(shape)
        return pl.BlockSpec((None,) + tuple(shape), lambda *_: index, pipeline_mode=pl.Buffered(1))

    def tiled(self, block, index_map):
        return pl.BlockSpec((None,) + tuple(block), lambda *a: (self.index,) + tuple(index_map(*a)))


def _mod_kernel(cc_ref, w_ref, b_ref, o_ref):
    cc = cc_ref[...]
    s = _silu(cc).astype(_BF)
    o_ref[...] = _dot(s, w_ref[...].astype(_BF)) + b_ref[...]


def _modulation(cc, w_mod, b_mod):
    depth, d, n = w_mod.shape
    tn = _tile(n, 1024)
    return pl.pallas_call(
        _mod_kernel,
        grid=(depth, n // tn),
        in_specs=[
            pl.BlockSpec((MOD_ROWS, d), lambda l, j: (0, 0)),
            pl.BlockSpec((None, d, tn), lambda l, j: (l, 0, j)),
            pl.BlockSpec((None, 1, tn), lambda l, j: (l, 0, j)),
        ],
        out_specs=pl.BlockSpec((None, MOD_ROWS, tn), lambda l, j: (l, 0, j)),
        out_shape=jax.ShapeDtypeStruct((depth, MOD_ROWS, n), _F32),
        compiler_params=_params("parallel", "parallel"),
        name="modulation",
    )(cc, w_mod, b_mod.reshape(depth, 1, n))


class _Stream:
    def __init__(self, n_rows, d, mod_row):
        self.n, self.d = n_rows, d
        self.tm = _tile(n_rows, ROW_TILE)
        self.tiles = n_rows // self.tm
        self.mod_row = mod_row

    def mod_spec(self, layer, part):
        return pl.BlockSpec((None, None, None, 1, self.d),
                            lambda i, *_: (layer, self.mod_row(i), part, 0, 0))


def _pw1_glu_kernel(x_ref, g_ref, sh_ref, sc_ref, w_ref, b_ref, u_ref, h_ref, *, tn):
    h_ref[...] = _rms_mod(x_ref[...], g_ref[...], sh_ref[...], sc_ref[...]).astype(_BF)
    d = h_ref.shape[1]
    for j in range(d // tn):
        lo, hi = j * tn, (j + 1) * tn
        a = _dot(h_ref[...], w_ref[:, lo:hi]) + b_ref[:, lo:hi]
        b = _dot(h_ref[...], w_ref[:, d + lo:d + hi]) + b_ref[:, d + lo:d + hi]
        u_ref[:, lo:hi] = (a * _sigmoid(b)).astype(u_ref.dtype)


def _pw1_glu(st, xs, mods, layer, g1, w_pw1, b_pw1):
    d, tm = st.d, st.tm
    return pl.pallas_call(
        functools.partial(_pw1_glu_kernel, tn=_tile(d, COL_TILE)),
        grid=(st.tiles,),
        in_specs=[
            pl.BlockSpec((tm, d), lambda i: (i, 0)),
            _resident((1, d)),
            st.mod_spec(layer, 0),
            st.mod_spec(layer, 1),
            w_pw1.resident(),
            _resident((1, 2 * d)),
        ],
        out_specs=pl.BlockSpec((tm, d), lambda i: (i, 0)),
        out_shape=jax.ShapeDtypeStruct((st.n, d), _BF),
        scratch_shapes=[pltpu.VMEM((tm, d), _BF)],
        compiler_params=_params("parallel"),
        name="pw1_glu",
    )(xs, g1.reshape(1, d), mods, mods, w_pw1.arr, b_pw1.reshape(1, 2 * d))


_CONV_LPAD = 16


def _conv_along_sublanes(pad_ref, w_ref, b_ref, n_rows, row_len, taps, store):
    tc = pad_ref.shape[-1]
    first = _CONV_LPAD - taps // 2

    def body(r, carry):
        for l0 in range(0, tc, LANES):
            lanes = pl.ds(l0, LANES)
            acc = jnp.zeros((row_len, LANES), _F32)
            for s in range(SUBLANES):
                ks = [k for k in range(taps) if (first + k) % SUBLANES == s]
                if not ks:
                    continue
                lo, hi = first + ks[0], first + ks[-1]
                n = row_len + hi - lo + (SUBLANES if s else 0)
                span = pad_ref[r, pl.ds(lo - s, n), lanes]
                if s:
                    span = pltpu.roll(span, n - s, 0)
                for k in ks:
                    off = first + k - lo
                    acc = acc + span[off:off + row_len] * w_ref[pl.ds(k, 1), lanes]
            store(r, l0, acc + b_ref[:, lanes])
        return carry
    lax.fori_loop(0, n_rows, body, 0)


def _dwconv_grid_kernel(u_ref, w_ref, b_ref, o_ref, padh_ref, padv_ref, *, n_h_tiles, taps):
    rows, _, tc = u_ref.shape
    half = taps // 2

    def store(r, l0, val):
        o_ref[r, :, pl.ds(l0, LANES)] = val.astype(o_ref.dtype)

    @pl.when(pl.program_id(1) < n_h_tiles)
    def _():
        zeros = jnp.zeros((rows, _CONV_LPAD, tc), _F32)
        padh_ref[:, pl.ds(0, _CONV_LPAD), :] = zeros
        padh_ref[:, pl.ds(_CONV_LPAD + GRID_W, _CONV_LPAD), :] = zeros
        padh_ref[:, pl.ds(_CONV_LPAD, GRID_W), :] = u_ref[...].astype(_F32)
        _conv_along_sublanes(padh_ref, w_ref, b_ref, rows, GRID_W, taps, store)

    @pl.when(pl.program_id(1) >= n_h_tiles)
    def _():
        zeros = jnp.zeros((half, GRID_W, tc), _F32)
        padv_ref[pl.ds(0, half)] = zeros
        padv_ref[pl.ds(half + rows, half)] = zeros
        padv_ref[pl.ds(half, rows)] = u_ref[...].astype(_F32)

        def body(r, carry):
            for l0 in range(0, tc, LANES):
                lanes = pl.ds(l0, LANES)
                acc = jnp.zeros((GRID_W, LANES), _F32)
                for k in range(taps):
                    acc = acc + padv_ref[r + k, :, lanes] * w_ref[pl.ds(k, 1), lanes]
                store(r, l0, acc + b_ref[:, lanes])
            return carry
        lax.fori_loop(0, rows, body, 0)


def _dwconv_seq_kernel(u_ref, w_ref, b_ref, o_ref, pad_ref, *, taps):
    n_seq, seq_len, tc = u_ref.shape
    zeros = jnp.zeros((n_seq, _CONV_LPAD, tc), _F32)
    pad_ref[:, pl.ds(0, _CONV_LPAD), :] = zeros
    pad_ref[:, pl.ds(_CONV_LPAD + seq_len, _CONV_LPAD), :] = zeros
    pad_ref[:, pl.ds(_CONV_LPAD, seq_len), :] = u_ref[...].astype(_F32)

    def store(r, l0, val):
        o_ref[r, :, pl.ds(l0, LANES)] = val.astype(o_ref.dtype)
    _conv_along_sublanes(pad_ref, w_ref, b_ref, n_seq, seq_len, taps, store)


def _dwconv(u, batch, seq, on_grid, w_dw, b_dw):
    n, d = u.shape
    taps = w_dw.shape[0]
    assert taps // 2 <= _CONV_LPAD
    tc = _tile(d // 2, 256)
    wspecs = [pl.BlockSpec((taps, tc), lambda i, c: (0, c)), pl.BlockSpec((1, tc), lambda i, c: (0, c))]
    if on_grid:
        assert seq % GRID_W == 0
        rows = seq // GRID_W
        block = pl.BlockSpec((rows, GRID_W, tc), lambda i, c: (i, 0, c))
        out = pl.pallas_call(
            functools.partial(_dwconv_grid_kernel, n_h_tiles=(d // 2) // tc, taps=taps),
            grid=(batch, d // tc),
            in_specs=[block] + wspecs,
            out_specs=block,
            out_shape=jax.ShapeDtypeStruct((n // GRID_W, GRID_W, d), u.dtype),
            scratch_shapes=[
                pltpu.VMEM((rows, GRID_W + 2 * _CONV_LPAD, tc), _F32),
                pltpu.VMEM((rows + 2 * (taps // 2), GRID_W, tc), _F32),
            ],
            compiler_params=_params("parallel", "parallel"),
            name="dwconv_grid",
        )(u.reshape(n // GRID_W, GRID_W, d), w_dw, b_dw.reshape(1, d))
    else:
        bb = _tile(batch, 8)
        block = pl.BlockSpec((bb, seq, tc), lambda i, c: (i, 0, c))
        out = pl.pallas_call(
            functools.partial(_dwconv_seq_kernel, taps=taps),
            grid=(batch // bb, d // tc),
            in_specs=[block] + wspecs,
            out_specs=block,
            out_shape=jax.ShapeDtypeStruct((batch, seq, d), u.dtype),
            scratch_shapes=[pltpu.VMEM((bb, seq + 2 * _CONV_LPAD, tc), _F32)],
            compiler_params=_params("parallel", "parallel"),
            name="dwconv_seq",
        )(u.reshape(batch, seq, d), w_dw, b_dw.reshape(1, d))
    return out.reshape(n, d)


def _ffn_step(h_ref, gate2_ref, wg_ref, wu_ref, wd_ref, fg_ref, o_ref, final_norm):
    h = h_ref[...]
    g = _dot(h, wg_ref[...])
    u = _dot(h, wu_ref[...])
    act = (_silu(g) * u).astype(_BF)
    o_ref[...] += gate2_ref[...] * _dot(act, wd_ref[...])
    if final_norm:
        @pl.when(pl.program_id(1) == pl.num_programs(1) - 1)
        def _():
            y = o_ref[...]
            o_ref[...] = y * lax.rsqrt(jnp.mean(y * y, axis=-1, keepdims=True) + EPS) * fg_ref[...]


def _conv_out_ffn_kernel(v_ref, lg_ref, lb_ref, w2_ref, b2_ref, x_ref, gate1_ref, g_ref, sh_ref, sc_ref,
                         gate2_ref, wg_ref, wu_ref, wd_ref, fg_ref, o_ref, h_ref, *, final_norm):
    @pl.when(pl.program_id(1) == 0)
    def _():
        v = v_ref[...].astype(_F32)
        mu = jnp.mean(v, axis=-1, keepdims=True)
        vc = v - mu
        var = jnp.mean(vc * vc, axis=-1, keepdims=True)
        y = vc * lax.rsqrt(var + EPS) * lg_ref[...] + lb_ref[...]
        z = _silu(y).astype(_BF)
        x1 = x_ref[...] + gate1_ref[...] * (_dot(z, w2_ref[...]) + b2_ref[...])
        o_ref[...] = x1
        h_ref[...] = _rms_mod(x1, g_ref[...], sh_ref[...], sc_ref[...]).astype(_BF)

    _ffn_step(h_ref, gate2_ref, wg_ref, wu_ref, wd_ref, fg_ref, o_ref, final_norm)


def _mlstm_out_ffn_kernel(z_ref, w2_ref, x_ref, gate1_ref, g_ref, sh_ref, sc_ref,
                          gate2_ref, wg_ref, wu_ref, wd_ref, fg_ref, o_ref, h_ref, *, final_norm):
    @pl.when(pl.program_id(1) == 0)
    def _():
        x1 = x_ref[...] + gate1_ref[...] * _dot(z_ref[...], w2_ref[...])
        o_ref[...] = x1
        h_ref[...] = _rms_mod(x1, g_ref[...], sh_ref[...], sc_ref[...]).astype(_BF)

    _ffn_step(h_ref, gate2_ref, wg_ref, wu_ref, wd_ref, fg_ref, o_ref, final_norm)


def _mix_out_ffn(st, mix_args, mix_specs, kern, xs, mods, layer, g2, w_gu, w_down, final_g, final_norm):
    d, tm = st.d, st.tm
    ff = w_down.shape[0]
    tf = _tile(ff, COL_TILE)
    nf = ff // tf
    row_block = pl.BlockSpec((tm, d), lambda i, f: (i, 0))
    return pl.pallas_call(
        functools.partial(kern, final_norm=final_norm),
        grid=(st.tiles, nf),
        in_specs=mix_specs + [
            row_block,
            st.mod_spec(layer, 2),
            _resident((1, d)),
            st.mod_spec(layer, 3),
            st.mod_spec(layer, 4),
            st.mod_spec(layer, 5),
            w_gu.tiled((d, tf), lambda i, f: (0, f)),
            w_gu.tiled((d, tf), lambda i, f: (0, nf + f)),
            w_down.tiled((tf, d), lambda i, f: (f, 0)),
            _resident((1, d)),
        ],
        out_specs=row_block,
        out_shape=jax.ShapeDtypeStruct((st.n, d), _F32),
        scratch_shapes=[pltpu.VMEM((tm, d), _BF)],
        compiler_params=_params("parallel", "arbitrary"),
        name=kern.__name__.strip("_").replace("_kernel", ""),
    )(*mix_args, xs, mods, g2.reshape(1, d), mods, mods, mods, w_gu.arr, w_gu.arr, w_down.arr,
      final_g.reshape(1, d))


def _conv_out_ffn(st, v, xs, mods, layer, ln_g, ln_b, w_pw2, b_pw2, g2, w_gu, w_down, final_g, final_norm):
    d = st.d
    specs = [pl.BlockSpec((st.tm, d), lambda i, f: (i, 0)), _resident((1, d)), _resident((1, d)),
             w_pw2.resident(), _resident((1, d))]
    args = (v, ln_g.reshape(1, d), ln_b.reshape(1, d), w_pw2.arr, b_pw2.reshape(1, d))
    return _mix_out_ffn(st, args, specs, _conv_out_ffn_kernel, xs, mods, layer, g2, w_gu, w_down,
                        final_g, final_norm)


def _mlstm_out_ffn(st, z, xs, mods, layer, w_out, g2, w_gu, w_down, final_g, final_norm):
    d = st.d
    specs = [pl.BlockSpec((st.tm, d), lambda i, f: (i, 0)), w_out.resident()]
    return _mix_out_ffn(st, (z, w_out.arr), specs, _mlstm_out_ffn_kernel, xs, mods, layer, g2, w_gu, w_down,
                        final_g, final_norm)


def _log_sigmoid(v):
    return jnp.minimum(v, 0.0) - jnp.log(1.0 + jnp.exp(-jnp.abs(v)))


def _prepare_gates(g, col_ref, row_ref, heads, chunk):
    rows = g.shape[0]
    g = GATE_CAP * jnp.tanh(g / GATE_CAP)
    lane = lax.broadcasted_iota(jnp.int32, g.shape, 1)
    is_f = (lane // heads) % 2 == 1
    vals = jnp.where(is_f, _log_sigmoid(g), g)
    t = lax.broadcasted_iota(jnp.int32, g.shape, 0) % chunk
    pre = vals
    suf = vals
    shift = 1
    while shift < chunk:
        pre = pre + jnp.where(t >= shift, pltpu.roll(pre, shift, 0), 0.0)
        suf = suf + jnp.where(t + shift < chunk, pltpu.roll(suf, rows - shift, 0), 0.0)
        shift *= 2
    lane_o = lax.broadcasted_iota(jnp.int32, (rows, LANES), 1)
    for h in range(heads):
        li_f = vals[:, h:h + 1]
        b_f = pre[:, heads + h:heads + h + 1]
        li_b = vals[:, 2 * heads + h:2 * heads + h + 1]
        b_b = suf[:, 3 * heads + h:3 * heads + h + 1]
        blk = jnp.where(lane_o == 0, li_f, jnp.where(lane_o == 1, b_f, jnp.where(lane_o == 2, li_b, b_b)))
        col_ref[h] = blk
        row_ref[pl.ds(h * SUBLANES, SUBLANES), :] = blk.T[0:SUBLANES, :]


def _min_proj_kernel(x_ref, g_ref, sh_ref, sc_ref, w_ref, wg_ref, bg_ref, qkvo_ref,
                     col_ref, row_ref, h_ref, *, tn, chunk, dqk, dv):
    h_ref[...] = _rms_mod(x_ref[...], g_ref[...], sh_ref[...], sc_ref[...]).astype(_BF)
    heads = qkvo_ref.shape[0]
    _prepare_gates(_dot(h_ref[...], wg_ref[...]) + bg_ref[...], col_ref, row_ref, heads, chunk)
    k_scale = float(dqk) ** -0.5
    pieces, start, lane0 = [], 0, 0
    for width, scale in ((dqk, None), (dqk, k_scale), (dv, None), (dv, None)):
        for h in range(heads):
            pieces.append((start, width, lane0, h, scale))
            start += width
        lane0 += width
    for lo in range(0, start, tn):
        hi = lo + tn
        acc = _dot(h_ref[...], w_ref[:, lo:hi])
        for ps, width, lane0, h, scale in pieces:
            a, b = max(lo, ps), min(hi, ps + width)
            if a < b:
                part = acc[:, a - lo:b - lo]
                qkvo_ref[h, :, lane0 + a - ps:lane0 + b - ps] = (part if scale is None else part * scale).astype(_BF)


def _min_proj(st, xs, mods, layer, g1, w_in, heads, dqk, dv, w_gates, b_gates, chunk):
    d, tm = st.d, st.tm
    n_main = heads * (2 * dqk + 2 * dv)
    assert tm % chunk == 0 and 4 * heads <= LANES
    bias = jnp.zeros((1, LANES), _F32).at[0, :4 * heads].set(b_gates)

    def head_major(width, dtype):
        return (pl.BlockSpec((heads, tm, width), lambda i: (0, i, 0)),
                jax.ShapeDtypeStruct((heads, st.n, width), dtype))
    outs = [head_major(2 * dqk + 2 * dv, _BF),
            head_major(LANES, _F32),
            (pl.BlockSpec((heads * SUBLANES, tm), lambda i: (0, i)),
             jax.ShapeDtypeStruct((heads * SUBLANES, st.n), _F32))]
    return pl.pallas_call(
        functools.partial(_min_proj_kernel, tn=_tile(n_main, COL_TILE), chunk=chunk, dqk=dqk, dv=dv),
        grid=(st.tiles,),
        in_specs=[
            pl.BlockSpec((tm, d), lambda i: (i, 0)),
            _resident((1, d)),
            st.mod_spec(layer, 0),
            st.mod_spec(layer, 1),
            w_in.resident((d, n_main)),
            w_gates.resident(),
            _resident((1, LANES)),
        ],
        out_specs=[o[0] for o in outs],
        out_shape=[o[1] for o in outs],
        scratch_shapes=[pltpu.VMEM((tm, d), _BF)],
        compiler_params=_params("parallel"),
        name="mlstm_in_proj",
    )(xs, g1.reshape(1, d), mods, mods, w_in.arr, w_gates.arr, bias)


def _scan_kernel(sx_ref, gcx_ref, grx_ref, sc_ref, gcc_ref, grc_ref, hg_ref,
                 zx_ref, zc_ref,
                 cf_ref, mf_ref, cb_ref, mb_ref, vax_ref, vac_ref, hfx_ref, hbx_ref, hfc_ref, hbc_ref, *,
                 chunk, dqk, dv):
    T = chunk

    def split(s_ref):
        return (s_ref.at[:, pl.ds(0, dqk)], s_ref.at[:, pl.ds(dqk, dqk)],
                s_ref.at[:, pl.ds(2 * dqk, dv)], s_ref.at[:, pl.ds(2 * dqk + dv, dv)])
    qx_ref, kx_ref, vx_ref, ox_ref = split(sx_ref)
    qc_ref, kc_ref, vc_ref, oc_ref = split(sc_ref)
    row_i = lax.broadcasted_iota(jnp.int32, (T, T), 0)
    col_i = lax.broadcasted_iota(jnp.int32, (T, T), 1)

    for ref in (cf_ref, mf_ref, cb_ref, mb_ref):
        ref[...] = jnp.zeros_like(ref)
    for v_ref, va_ref in ((vx_ref, vax_ref), (vc_ref, vac_ref)):
        va_ref[:, :dv] = v_ref[...]
        va_ref[:, dv:] = jnp.ones((va_ref.shape[0], LANES), _BF)

    def step(q_ref, k_ref, va_ref, gc_ref, gr_ref, r0, c_ref, m_ref, backward):
        rows = slice(r0, r0 + T)
        q = q_ref[rows, :]
        k = k_ref[rows, :]
        va = va_ref[rows, :]
        gcb = gc_ref[rows, :]
        grb = gr_ref[:, rows]
        o = 2 if backward else 0
        li_rep = jnp.broadcast_to(gcb[:, o:o + 1], (T, LANES))
        b_rep = jnp.broadcast_to(gcb[:, o + 1:o + 2], (T, LANES))
        li_r, b_r = grb[o:o + 1, :], grb[o + 1:o + 2, :]
        m = m_ref[0:1, :]
        a_rep = b_rep + m
        mt_rep = jnp.maximum(a_rep, jnp.max(li_r, axis=1, keepdims=True))
        w_inter = jnp.exp(a_rep - mt_rep)
        d = _lane_tile(b_rep - mt_rep, T) + (li_r - b_r)
        p = jnp.exp(jnp.where((row_i <= col_i) if backward else (row_i >= col_i), d, -jnp.inf))
        qk = lax.dot_general(q, k, (((1,), (1,)), ((), ())), preferred_element_type=_F32)
        s = (qk * p).astype(_BF)
        c_old = c_ref[...]
        num = _lane_tile(w_inter, dv + LANES) * _dot(q, c_old.astype(_BF)) + _dot(s, va)
        den = num[:, dv:]
        inv = 1.0 / jnp.maximum(jnp.abs(den), jnp.exp(-mt_rep))
        h = num[:, :dv] * _lane_tile(inv, dv)
        b_tot = b_rep[0:1, :] if backward else b_rep[T - 1:T, :]
        gl = b_tot - b_rep + li_rep
        m_new = jnp.maximum(b_tot + m, jnp.max(gl, axis=0, keepdims=True))
        decay = jnp.exp(b_tot + m - m_new)
        kw = (k.astype(_F32) * _lane_tile(jnp.exp(gl - m_new), dqk)).astype(_BF)
        c_ref[...] = _lane_tile(decay, dv + LANES) * c_old + lax.dot_general(
            kw, va, (((0,), (0,)), ((), ())), preferred_element_type=_F32)
        m_ref[...] = jnp.broadcast_to(m_new, m_ref.shape)
        return h

    def run(q_ref, k_ref, va_ref, o_ref, gc_ref, gr_ref, hf_ref, hb_ref, z_ref):
        n_chunks = q_ref.shape[0] // T

        def finalize(ci):
            rows = slice(ci * T, (ci + 1) * T)
            hs = hf_ref[rows, :] + hb_ref[rows, :]
            hn = hs * lax.rsqrt(jnp.mean(hs * hs, axis=-1, keepdims=True) + EPS)
            gate = _sigmoid(o_ref[rows, :].astype(_F32))
            z_ref[rows, :] = (hn * hg_ref[...] * gate).astype(_BF)

        for i in range(n_chunks):
            jf, jb = i, n_chunks - 1 - i
            hf_ref[jf * T:(jf + 1) * T, :] = step(q_ref, k_ref, va_ref, gc_ref, gr_ref, jf * T, cf_ref, mf_ref, False)
            hb_ref[jb * T:(jb + 1) * T, :] = step(q_ref, k_ref, va_ref, gc_ref, gr_ref, jb * T, cb_ref, mb_ref, True)
            if jf >= jb:
                for ci in sorted({jf, jb}):
                    finalize(ci)

    run(qc_ref, kc_ref, vac_ref, oc_ref, gcc_ref, grc_ref, hfc_ref, hbc_ref, zc_ref)
    run(qx_ref, kx_ref, vax_ref, ox_ref, gcx_ref, grx_ref, hfx_ref, hbx_ref, zx_ref)


def _scan(batch, seq, ctx_len, d, qkvo_x, gcol_x, grow_x, qkvo_c, gcol_c, grow_c, hn_g, heads, dqk, chunk):
    dv = d // heads
    assert qkvo_x.shape[2] == 2 * dqk + 2 * dv
    assert seq % chunk == 0 and ctx_len % chunk == 0 and chunk % LANES == 0 and dv % LANES == 0

    def stream_specs(length):
        def blk(width):
            return pl.BlockSpec((None, length, width), lambda b, h: (h, b, 0))
        return [blk(2 * dqk + 2 * dv), blk(LANES), pl.BlockSpec((SUBLANES, length), lambda b, h: (h, b))]

    return pl.pallas_call(
        functools.partial(_scan_kernel, chunk=chunk, dqk=dqk, dv=dv),
        grid=(batch, heads),
        in_specs=stream_specs(seq) + stream_specs(ctx_len) + [pl.BlockSpec((1, dv), lambda b, h: (0, h))],
        out_specs=[
            pl.BlockSpec((seq, dv), lambda b, h: (b, h)),
            pl.BlockSpec((ctx_len, dv), lambda b, h: (b, h)),
        ],
        out_shape=[
            jax.ShapeDtypeStruct((batch * seq, d), _BF),
            jax.ShapeDtypeStruct((batch * ctx_len, d), _BF),
        ],
        scratch_shapes=[
            pltpu.VMEM((dqk, dv + LANES), _F32), pltpu.VMEM((SUBLANES, LANES), _F32),
            pltpu.VMEM((dqk, dv + LANES), _F32), pltpu.VMEM((SUBLANES, LANES), _F32),
            pltpu.VMEM((seq, dv + LANES), _BF), pltpu.VMEM((ctx_len, dv + LANES), _BF),
            pltpu.VMEM((seq, dv), _F32), pltpu.VMEM((seq, dv), _F32),
            pltpu.VMEM((ctx_len, dv), _F32), pltpu.VMEM((ctx_len, dv), _F32),
        ],
        compiler_params=_params("parallel", "parallel"),
        name="mlstm_scan",
    )(qkvo_x, gcol_x, grow_x, qkvo_c, gcol_c, grow_c, hn_g.reshape(1, d))


def kernel(x, c, ctx, c_ctx, norm1_g, norm2_g, w_mod, b_mod, w_gu, w_down, conv_w_pw1, conv_b_pw1,
           conv_w_dw, conv_b_dw, conv_ln_g, conv_ln_b, conv_w_pw2, conv_b_pw2, m_w_in, m_b_gates,
           m_hn_g, m_w_out, final_g):
    batch, seq, d = x.shape
    ctx_len = ctx.shape[1]
    depth = w_mod.shape[0]
    heads = max(4, d // 512)
    dqk_all = d // 2
    chunk = _tile(ctx_len, 256)
    assert batch + 1 <= MOD_ROWS and 4 * heads <= LANES

    sx = _Stream(batch * seq, d, None)
    assert seq % sx.tm == 0
    tiles_per_sample = seq // sx.tm
    sx.mod_row = lambda i: i // tiles_per_sample
    sc = _Stream(batch * ctx_len, d, lambda i: batch)

    cc = jnp.zeros((MOD_ROWS, d), _F32).at[:batch].set(c).at[batch].set(c_ctx)
    mods = _modulation(cc, w_mod, b_mod).reshape(depth, MOD_ROWS, 6, 1, d)

    w_gu_b, w_down_b = w_gu.astype(_BF), w_down.astype(_BF)
    w_pw1_b, w_pw2_b = conv_w_pw1.astype(_BF), conv_w_pw2.astype(_BF)
    n_main = 2 * dqk_all + 2 * d
    w_in_b = m_w_in.astype(_BF)
    w_gates_b = jnp.zeros((m_w_in.shape[0], d, LANES), _BF).at[:, :, :4 * heads].set(
        m_w_in[:, :, n_main:].astype(_BF))
    w_out_b = m_w_out.astype(_BF)

    xs = x.reshape(batch * seq, d)
    cs = ctx.reshape(batch * ctx_len, d)
    for i in range(depth):
        last = i == depth - 1
        j = i // 2
        ffn_w = (norm2_g[i], _Layered(w_gu_b, i), _Layered(w_down_b, i), final_g)
        if i % 2 == 0:
            conv_w = (conv_ln_g[j], conv_ln_b[j], _Layered(w_pw2_b, j), conv_b_pw2[j])
            w_pw1 = _Layered(w_pw1_b, j)
            ux = _pw1_glu(sx, xs, mods, i, norm1_g[i], w_pw1, conv_b_pw1[j])
            vx = _dwconv(ux, batch, seq, True, conv_w_dw[j], conv_b_dw[j])
            if not last:
                uc = _pw1_glu(sc, cs, mods, i, norm1_g[i], w_pw1, conv_b_pw1[j])
                vc = _dwconv(uc, batch, ctx_len, False, conv_w_dw[j], conv_b_dw[j])
                cs = _conv_out_ffn(sc, vc, cs, mods, i, *conv_w, *ffn_w, False)
            xs = _conv_out_ffn(sx, vx, xs, mods, i, *conv_w, *ffn_w, last)
        else:
            in_w = (_Layered(w_in_b, j), heads, dqk_all // heads, d // heads, _Layered(w_gates_b, j),
                    m_b_gates[j], chunk)
            w_out = _Layered(w_out_b, j)
            px, gcol_x, grow_x = _min_proj(sx, xs, mods, i, norm1_g[i], *in_w)
            pc, gcol_c, grow_c = _min_proj(sc, cs, mods, i, norm1_g[i], *in_w)
            zx, zc = _scan(batch, seq, ctx_len, d, px, gcol_x, grow_x, pc, gcol_c, grow_c,
                           m_hn_g[j], heads, dqk_all // heads, chunk)
            if not last:
                cs = _mlstm_out_ffn(sc, zc, cs, mods, i, w_out, *ffn_w, False)
            xs = _mlstm_out_ffn(sx, zx, xs, mods, i, w_out, *ffn_w, last)
    return xs.reshape(batch, seq, d)
```

```python
import functools

import jax
import jax.numpy as jnp
from jax import lax
from jax.experimental import pallas as pl
from jax.experimental.pallas import tpu as pltpu

EPS = 1e-6
GATE_CAP = 15.0
GRID_W = 64
LANES = 128
SUBLANES = 8
VMEM_LIMIT = 56 * 1024 * 1024
MOD_ROWS = 32
ROW_TILE = 512
COL_TILE = 512
MXU_TILE = 256
MOD_COL_TILE = 1024
CTX_CONV_SEQS = 8

_BF = jnp.bfloat16
_F32 = jnp.float32


def _tile(dim, pref):
    t = min(dim, pref)
    while dim % t:
        t //= 2
    return t


def _params(*sem):
    return pltpu.CompilerParams(dimension_semantics=sem, vmem_limit_bytes=VMEM_LIMIT)


def _sigmoid(v):
    return 0.5 * (1.0 + jnp.tanh(0.5 * v))


def _silu(v):
    h = 0.5 * v
    return h + h * jnp.tanh(h)


def _lane_tile(rep, width):
    if width % LANES == 0:
        return rep if width == LANES else jnp.concatenate([rep] * (width // LANES), axis=1)
    assert width < LANES
    return rep[:, :width]


def _dot(a, b):
    return jnp.dot(a, b, preferred_element_type=_F32)


def _rms_mod(x, g, shift, scale):
    y = x * lax.rsqrt(jnp.mean(x * x, axis=-1, keepdims=True) + EPS)
    return y * (g * (1.0 + scale)) + shift


def _resident(shape):
    zeros = (0,) * len(shape)
    return pl.BlockSpec(shape, lambda *_: zeros, pipeline_mode=pl.Buffered(1))


class _Layered:
    def __init__(self, stacked, index):
        self.arr, self.index = stacked, index
        self.shape = stacked.shape[1:]

    def resident(self, shape=None):
        shape = self.shape if shape is None else shape
        index = (self.index,) + (0,) * len(shape)
        return pl.BlockSpec((None,) + tuple(shape), lambda *_: index, pipeline_mode=pl.Buffered(1))

    def tiled(self, block, index_map):
        return pl.BlockSpec((None,) + tuple(block), lambda *a: (self.index,) + tuple(index_map(*a)))


def _mod_kernel(cc_ref, w_ref, b_ref, o_ref):
    cc = cc_ref[...]
    s = _silu(cc).astype(_BF)
    o_ref[...] = _dot(s, w_ref[...].astype(_BF)) + b_ref[...]


def _modulation(cc, w_mod, b_mod):
    depth, d, n = w_mod.shape
    tn = _tile(n, MOD_COL_TILE)
    return pl.pallas_call(
        _mod_kernel,
        grid=(depth, n // tn),
        in_specs=[
            pl.BlockSpec((MOD_ROWS, d), lambda l, j: (0, 0)),
            pl.BlockSpec((None, d, tn), lambda l, j: (l, 0, j)),
            pl.BlockSpec((None, 1, tn), lambda l, j: (l, 0, j)),
        ],
        out_specs=pl.BlockSpec((None, MOD_ROWS, tn), lambda l, j: (l, 0, j)),
        out_shape=jax.ShapeDtypeStruct((depth, MOD_ROWS, n), _F32),
        compiler_params=_params("parallel", "parallel"),
        name="modulation",
    )(cc, w_mod, b_mod.reshape(depth, 1, n))


class _Stream:
    def __init__(self, n_rows, d, mod_row):
        self.n, self.d = n_rows, d
        self.tm = _tile(n_rows, ROW_TILE)
        self.tiles = n_rows // self.tm
        self.mod_row = mod_row

    def mod_spec(self, layer, part):
        return pl.BlockSpec((None, None, None, 1, self.d),
                            lambda i, *_: (layer, self.mod_row(i), part, 0, 0))


def _pw1_glu_kernel(x_ref, g_ref, sh_ref, sc_ref, w_ref, b_ref, u_ref, h_ref, *, tn):
    h_ref[...] = _rms_mod(x_ref[...], g_ref[...], sh_ref[...], sc_ref[...]).astype(_BF)
    d = h_ref.shape[1]
    for j in range(d // tn):
        lo, hi = j * tn, (j + 1) * tn
        a = _dot(h_ref[...], w_ref[:, lo:hi]) + b_ref[:, lo:hi]
        b = _dot(h_ref[...], w_ref[:, d + lo:d + hi]) + b_ref[:, d + lo:d + hi]
        u_ref[:, lo:hi] = (a * _sigmoid(b)).astype(u_ref.dtype)


def _pw1_glu(st, xs, mods, layer, g1, w_pw1, b_pw1):
    d, tm = st.d, st.tm
    return pl.pallas_call(
        functools.partial(_pw1_glu_kernel, tn=_tile(d, COL_TILE)),
        grid=(st.tiles,),
        in_specs=[
            pl.BlockSpec((tm, d), lambda i: (i, 0)),
            _resident((1, d)),
            st.mod_spec(layer, 0),
            st.mod_spec(layer, 1),
            w_pw1.resident(),
            _resident((1, 2 * d)),
        ],
        out_specs=pl.BlockSpec((tm, d), lambda i: (i, 0)),
        out_shape=jax.ShapeDtypeStruct((st.n, d), _BF),
        scratch_shapes=[pltpu.VMEM((tm, d), _BF)],
        compiler_params=_params("parallel"),
        name="pw1_glu",
    )(xs, g1.reshape(1, d), mods, mods, w_pw1.arr, b_pw1.reshape(1, 2 * d))


_CONV_LPAD = 16


def _conv_along_sublanes(pad_ref, w_ref, b_ref, n_rows, row_len, taps, store):
    tc = pad_ref.shape[-1]
    first = _CONV_LPAD - taps // 2

    def body(r, carry):
        for l0 in range(0, tc, LANES):
            lanes = pl.ds(l0, LANES)
            acc = jnp.broadcast_to(b_ref[:, lanes], (row_len, LANES))
            for s in range(SUBLANES):
                ks = [k for k in range(taps) if (first + k) % SUBLANES == s]
                if not ks:
                    continue
                lo, hi = first + ks[0], first + ks[-1]
                n = row_len + hi - lo + (SUBLANES if s else 0)
                span = pad_ref[r, pl.ds(lo - s, n), lanes]
                if s:
                    span = pltpu.roll(span, n - s, 0)
                for k in ks:
                    off = first + k - lo
                    acc = acc + span[off:off + row_len] * w_ref[pl.ds(k, 1), lanes]
            store(r, l0, acc)
        return carry
    lax.fori_loop(0, n_rows, body, 0)


def _dwconv_grid_kernel(u_ref, w_ref, b_ref, o_ref, padh_ref, padv_ref, *, n_h_tiles, taps):
    rows, _, tc = u_ref.shape
    half = taps // 2

    def store(r, l0, val):
        o_ref[r, :, pl.ds(l0, LANES)] = val.astype(o_ref.dtype)

    @pl.when(pl.program_id(1) < n_h_tiles)
    def _():
        zeros = jnp.zeros((rows, _CONV_LPAD, tc), _F32)
        padh_ref[:, pl.ds(0, _CONV_LPAD), :] = zeros
        padh_ref[:, pl.ds(_CONV_LPAD + GRID_W, _CONV_LPAD), :] = zeros
        padh_ref[:, pl.ds(_CONV_LPAD, GRID_W), :] = u_ref[...].astype(_F32)
        _conv_along_sublanes(padh_ref, w_ref, b_ref, rows, GRID_W, taps, store)

    @pl.when(pl.program_id(1) >= n_h_tiles)
    def _():
        zeros = jnp.zeros((half, GRID_W, tc), _F32)
        padv_ref[pl.ds(0, half)] = zeros
        padv_ref[pl.ds(half + rows, half)] = zeros
        padv_ref[pl.ds(half, rows)] = u_ref[...].astype(_F32)

        def body(r, carry):
            for l0 in range(0, tc, LANES):
                lanes = pl.ds(l0, LANES)
                acc = jnp.broadcast_to(b_ref[:, lanes], (GRID_W, LANES))
                for k in range(taps):
                    acc = acc + padv_ref[r + k, :, lanes] * w_ref[pl.ds(k, 1), lanes]
                store(r, l0, acc)
            return carry
        lax.fori_loop(0, rows, body, 0)


def _dwconv_seq_kernel(u_ref, w_ref, b_ref, o_ref, pad_ref, *, taps):
    n_seq, seq_len, tc = u_ref.shape
    zeros = jnp.zeros((n_seq, _CONV_LPAD, tc), _F32)
    pad_ref[:, pl.ds(0, _CONV_LPAD), :] = zeros
    pad_ref[:, pl.ds(_CONV_LPAD + seq_len, _CONV_LPAD), :] = zeros
    pad_ref[:, pl.ds(_CONV_LPAD, seq_len), :] = u_ref[...].astype(_F32)

    def store(r, l0, val):
        o_ref[r, :, pl.ds(l0, LANES)] = val.astype(o_ref.dtype)
    _conv_along_sublanes(pad_ref, w_ref, b_ref, n_seq, seq_len, taps, store)


def _dwconv(u, batch, seq, on_grid, w_dw, b_dw):
    n, d = u.shape
    taps = w_dw.shape[0]
    assert taps // 2 <= _CONV_LPAD
    tc = _tile(d // 2, MXU_TILE)
    wspecs = [pl.BlockSpec((taps, tc), lambda i, c: (0, c)), pl.BlockSpec((1, tc), lambda i, c: (0, c))]
    if on_grid:
        assert seq % GRID_W == 0
        rows = seq // GRID_W
        block = pl.BlockSpec((rows, GRID_W, tc), lambda i, c: (i, 0, c))
        out = pl.pallas_call(
            functools.partial(_dwconv_grid_kernel, n_h_tiles=(d // 2) // tc, taps=taps),
            grid=(batch, d // tc),
            in_specs=[block] + wspecs,
            out_specs=block,
            out_shape=jax.ShapeDtypeStruct((n // GRID_W, GRID_W, d), u.dtype),
            scratch_shapes=[
                pltpu.VMEM((rows, GRID_W + 2 * _CONV_LPAD, tc), _F32),
                pltpu.VMEM((rows + 2 * (taps // 2), GRID_W, tc), _F32),
            ],
            compiler_params=_params("parallel", "parallel"),
            name="dwconv_grid",
        )(u.reshape(n // GRID_W, GRID_W, d), w_dw, b_dw.reshape(1, d))
    else:
        bb = _tile(batch, CTX_CONV_SEQS)
        block = pl.BlockSpec((bb, seq, tc), lambda i, c: (i, 0, c))
        out = pl.pallas_call(
            functools.partial(_dwconv_seq_kernel, taps=taps),
            grid=(batch // bb, d // tc),
            in_specs=[block] + wspecs,
            out_specs=block,
            out_shape=jax.ShapeDtypeStruct((batch, seq, d), u.dtype),
            scratch_shapes=[pltpu.VMEM((bb, seq + 2 * _CONV_LPAD, tc), _F32)],
            compiler_params=_params("parallel", "parallel"),
            name="dwconv_seq",
        )(u.reshape(batch, seq, d), w_dw, b_dw.reshape(1, d))
    return out.reshape(n, d)


def _ffn_step(h_ref, gate2_ref, wg_ref, wu_ref, wd_ref, fg_ref, o_ref, final_norm):
    h = h_ref[...]
    g = _dot(h, wg_ref[...])
    u = _dot(h, wu_ref[...])
    act = (_silu(g) * u).astype(_BF)
    o_ref[...] += gate2_ref[...] * _dot(act, wd_ref[...])
    if final_norm:
        @pl.when(pl.program_id(1) == pl.num_programs(1) - 1)
        def _():
            y = o_ref[...]
            o_ref[...] = y * lax.rsqrt(jnp.mean(y * y, axis=-1, keepdims=True) + EPS) * fg_ref[...]


def _conv_out_ffn_kernel(v_ref, lg_ref, lb_ref, w2_ref, b2_ref, x_ref, gate1_ref, g_ref, sh_ref, sc_ref,
                         gate2_ref, wg_ref, wu_ref, wd_ref, fg_ref, o_ref, h_ref, *, final_norm):
    @pl.when(pl.program_id(1) == 0)
    def _():
        v = v_ref[...].astype(_F32)
        mu = jnp.mean(v, axis=-1, keepdims=True)
        vc = v - mu
        var = jnp.mean(vc * vc, axis=-1, keepdims=True)
        y = vc * lax.rsqrt(var + EPS) * lg_ref[...] + lb_ref[...]
        z = _silu(y).astype(_BF)
        x1 = x_ref[...] + gate1_ref[...] * (_dot(z, w2_ref[...]) + b2_ref[...])
        o_ref[...] = x1
        h_ref[...] = _rms_mod(x1, g_ref[...], sh_ref[...], sc_ref[...]).astype(_BF)

    _ffn_step(h_ref, gate2_ref, wg_ref, wu_ref, wd_ref, fg_ref, o_ref, final_norm)


def _mlstm_out_ffn_kernel(z_ref, w2_ref, x_ref, gate1_ref, g_ref, sh_ref, sc_ref,
                          gate2_ref, wg_ref, wu_ref, wd_ref, fg_ref, o_ref, h_ref, *, final_norm):
    @pl.when(pl.program_id(1) == 0)
    def _():
        x1 = x_ref[...] + gate1_ref[...] * _dot(z_ref[...], w2_ref[...])
        o_ref[...] = x1
        h_ref[...] = _rms_mod(x1, g_ref[...], sh_ref[...], sc_ref[...]).astype(_BF)

    _ffn_step(h_ref, gate2_ref, wg_ref, wu_ref, wd_ref, fg_ref, o_ref, final_norm)


def _mix_out_ffn(st, mix_args, mix_specs, kern, xs, mods, layer, g2, w_gu, w_down, final_g, final_norm):
    d, tm = st.d, st.tm
    ff = w_down.shape[0]
    tf = _tile(ff, COL_TILE)
    nf = ff // tf
    row_block = pl.BlockSpec((tm, d), lambda i, f: (i, 0))
    return pl.pallas_call(
        functools.partial(kern, final_norm=final_norm),
        grid=(st.tiles, nf),
        in_specs=mix_specs + [
            row_block,
            st.mod_spec(layer, 2),
            _resident((1, d)),
            st.mod_spec(layer, 3),
            st.mod_spec(layer, 4),
            st.mod_spec(layer, 5),
            w_gu.tiled((d, tf), lambda i, f: (0, f)),
            w_gu.tiled((d, tf), lambda i, f: (0, nf + f)),
            w_down.tiled((tf, d), lambda i, f: (f, 0)),
            _resident((1, d)),
        ],
        out_specs=row_block,
        out_shape=jax.ShapeDtypeStruct((st.n, d), _F32),
        scratch_shapes=[pltpu.VMEM((tm, d), _BF)],
        compiler_params=_params("parallel", "arbitrary"),
        name=kern.__name__.strip("_").replace("_kernel", ""),
    )(*mix_args, xs, mods, g2.reshape(1, d), mods, mods, mods, w_gu.arr, w_gu.arr, w_down.arr,
      final_g.reshape(1, d))


def _conv_out_ffn(st, v, xs, mods, layer, ln_g, ln_b, w_pw2, b_pw2, g2, w_gu, w_down, final_g, final_norm):
    d = st.d
    specs = [pl.BlockSpec((st.tm, d), lambda i, f: (i, 0)), _resident((1, d)), _resident((1, d)),
             w_pw2.resident(), _resident((1, d))]
    args = (v, ln_g.reshape(1, d), ln_b.reshape(1, d), w_pw2.arr, b_pw2.reshape(1, d))
    return _mix_out_ffn(st, args, specs, _conv_out_ffn_kernel, xs, mods, layer, g2, w_gu, w_down,
                        final_g, final_norm)


def _mlstm_out_ffn(st, z, xs, mods, layer, w_out, g2, w_gu, w_down, final_g, final_norm):
    d = st.d
    specs = [pl.BlockSpec((st.tm, d), lambda i, f: (i, 0)), w_out.resident()]
    return _mix_out_ffn(st, (z, w_out.arr), specs, _mlstm_out_ffn_kernel, xs, mods, layer, g2, w_gu, w_down,
                        final_g, final_norm)


def _log_sigmoid(v):
    return jnp.minimum(v, 0.0) - jnp.log(1.0 + jnp.exp(-jnp.abs(v)))


def _prepare_gates(g, col_ref, row_ref, heads, chunk):
    rows = g.shape[0]
    g = GATE_CAP * jnp.tanh(g / GATE_CAP)
    lane = lax.broadcasted_iota(jnp.int32, g.shape, 1)
    is_f = (lane // heads) % 2 == 1
    vals = jnp.where(is_f, _log_sigmoid(g), g)
    t = lax.broadcasted_iota(jnp.int32, g.shape, 0) % chunk
    pre = vals
    suf = vals
    shift = 1
    while shift < chunk:
        pre = pre + jnp.where(t >= shift, pltpu.roll(pre, shift, 0), 0.0)
        suf = suf + jnp.where(t + shift < chunk, pltpu.roll(suf, rows - shift, 0), 0.0)
        shift *= 2
    lane_o = lax.broadcasted_iota(jnp.int32, (rows, LANES), 1)
    for h in range(heads):
        li_f = vals[:, h:h + 1]
        b_f = pre[:, heads + h:heads + h + 1]
        li_b = vals[:, 2 * heads + h:2 * heads + h + 1]
        b_b = suf[:, 3 * heads + h:3 * heads + h + 1]
        blk = jnp.where(lane_o == 0, li_f, jnp.where(lane_o == 1, b_f, jnp.where(lane_o == 2, li_b, b_b)))
        col_ref[h] = blk
        row_ref[pl.ds(h * SUBLANES, SUBLANES), :] = blk.T[0:SUBLANES, :]


def _min_proj_kernel(x_ref, g_ref, sh_ref, sc_ref, w_ref, wg_ref, bg_ref, qkvo_ref,
                     col_ref, row_ref, h_ref, *, tn, chunk, dqk, dv):
    h_ref[...] = _rms_mod(x_ref[...], g_ref[...], sh_ref[...], sc_ref[...]).astype(_BF)
    heads = qkvo_ref.shape[0]
    _prepare_gates(_dot(h_ref[...], wg_ref[...]) + bg_ref[...], col_ref, row_ref, heads, chunk)
    k_scale = float(dqk) ** -0.5
    pieces, start, lane0 = [], 0, 0
    for width, scale in ((dqk, None), (dqk, k_scale), (dv, None), (dv, None)):
        for h in range(heads):
            pieces.append((start, width, lane0, h, scale))
            start += width
        lane0 += width
    for lo in range(0, start, tn):
        hi = lo + tn
        acc = _dot(h_ref[...], w_ref[:, lo:hi])
        for ps, width, lane0, h, scale in pieces:
            a, b = max(lo, ps), min(hi, ps + width)
            if a < b:
                part = acc[:, a - lo:b - lo]
                qkvo_ref[h, :, lane0 + a - ps:lane0 + b - ps] = (part if scale is None else part * scale).astype(_BF)


def _min_proj(st, xs, mods, layer, g1, w_in, heads, dqk, dv, w_gates, b_gates, chunk):
    d, tm = st.d, st.tm
    n_main = heads * (2 * dqk + 2 * dv)
    assert tm % chunk == 0 and 4 * heads <= LANES
    bias = jnp.zeros((1, LANES), _F32).at[0, :4 * heads].set(b_gates)

    def head_major(width, dtype):
        return (pl.BlockSpec((heads, tm, width), lambda i: (0, i, 0)),
                jax.ShapeDtypeStruct((heads, st.n, width), dtype))
    outs = [head_major(2 * dqk + 2 * dv, _BF),
            head_major(LANES, _F32),
            (pl.BlockSpec((heads * SUBLANES, tm), lambda i: (0, i)),
             jax.ShapeDtypeStruct((heads * SUBLANES, st.n), _F32))]
    return pl.pallas_call(
        functools.partial(_min_proj_kernel, tn=_tile(n_main, COL_TILE), chunk=chunk, dqk=dqk, dv=dv),
        grid=(st.tiles,),
        in_specs=[
            pl.BlockSpec((tm, d), lambda i: (i, 0)),
            _resident((1, d)),
            st.mod_spec(layer, 0),
            st.mod_spec(layer, 1),
            w_in.resident((d, n_main)),
            w_gates.resident(),
            _resident((1, LANES)),
        ],
        out_specs=[o[0] for o in outs],
        out_shape=[o[1] for o in outs],
        scratch_shapes=[pltpu.VMEM((tm, d), _BF)],
        compiler_params=_params("parallel"),
        name="mlstm_in_proj",
    )(xs, g1.reshape(1, d), mods, mods, w_in.arr, w_gates.arr, bias)


def _scan_kernel(sx_ref, gcx_ref, grx_ref, sc_ref, gcc_ref, grc_ref, hg_ref,
                 zx_ref, zc_ref,
                 cf_ref, mf_ref, cb_ref, mb_ref, vax_ref, vac_ref, hfx_ref, hbx_ref, hfc_ref, hbc_ref, *,
                 chunk, dqk, dv):
    T = chunk

    def split(s_ref):
        return (s_ref.at[:, pl.ds(0, dqk)], s_ref.at[:, pl.ds(dqk, dqk)],
                s_ref.at[:, pl.ds(2 * dqk, dv)], s_ref.at[:, pl.ds(2 * dqk + dv, dv)])
    qx_ref, kx_ref, vx_ref, ox_ref = split(sx_ref)
    qc_ref, kc_ref, vc_ref, oc_ref = split(sc_ref)
    row_i = lax.broadcasted_iota(jnp.int32, (T, T), 0)
    col_i = lax.broadcasted_iota(jnp.int32, (T, T), 1)

    for ref in (cf_ref, mf_ref, cb_ref, mb_ref):
        ref[...] = jnp.zeros_like(ref)
    for v_ref, va_ref in ((vx_ref, vax_ref), (vc_ref, vac_ref)):
        va_ref[:, :dv] = v_ref[...]
        va_ref[:, dv:] = jnp.ones((va_ref.shape[0], LANES), _BF)

    def step(q_ref, k_ref, va_ref, gc_ref, gr_ref, r0, c_ref, m_ref, backward):
        rows = slice(r0, r0 + T)
        q = q_ref[rows, :]
        k = k_ref[rows, :]
        va = va_ref[rows, :]
        gcb = gc_ref[rows, :]
        grb = gr_ref[:, rows]
        o = 2 if backward else 0
        li_rep = jnp.broadcast_to(gcb[:, o:o + 1], (T, LANES))
        b_rep = jnp.broadcast_to(gcb[:, o + 1:o + 2], (T, LANES))
        li_r, b_r = grb[o:o + 1, :], grb[o + 1:o + 2, :]
        m = m_ref[0:1, :]
        a_rep = b_rep + m
        mt_rep = jnp.maximum(a_rep, jnp.max(li_r, axis=1, keepdims=True))
        w_inter = jnp.exp(a_rep - mt_rep)
        d = _lane_tile(b_rep - mt_rep, T) + (li_r - b_r)
        p = jnp.exp(jnp.where((row_i <= col_i) if backward else (row_i >= col_i), d, -jnp.inf))
        qk = lax.dot_general(q, k, (((1,), (1,)), ((), ())), preferred_element_type=_F32)
        s = (qk * p).astype(_BF)
        c_old = c_ref[...]
        num = _lane_tile(w_inter, dv + LANES) * _dot(q, c_old.astype(_BF)) + _dot(s, va)
        den = num[:, dv:]
        inv = 1.0 / jnp.maximum(jnp.abs(den), jnp.exp(-mt_rep))
        h = num[:, :dv] * _lane_tile(inv, dv)
        b_tot = b_rep[0:1, :] if backward else b_rep[T - 1:T, :]
        gl = b_tot - b_rep + li_rep
        m_new = jnp.maximum(b_tot + m, jnp.max(gl, axis=0, keepdims=True))
        decay = jnp.exp(b_tot + m - m_new)
        kw = (k.astype(_F32) * _lane_tile(jnp.exp(gl - m_new), dqk)).astype(_BF)
        c_ref[...] = _lane_tile(decay, dv + LANES) * c_old + lax.dot_general(
            kw, va, (((0,), (0,)), ((), ())), preferred_element_type=_F32)
        m_ref[...] = jnp.broadcast_to(m_new, m_ref.shape)
        return h

    def run(q_ref, k_ref, va_ref, o_ref, gc_ref, gr_ref, hf_ref, hb_ref, z_ref):
        n_chunks = q_ref.shape[0] // T

        def finalize(ci):
            rows = slice(ci * T, (ci + 1) * T)
            hs = hf_ref[rows, :] + hb_ref[rows, :]
            hn = hs * lax.rsqrt(jnp.mean(hs * hs, axis=-1, keepdims=True) + EPS)
            gate = _sigmoid(o_ref[rows, :].astype(_F32))
            z_ref[rows, :] = (hn * hg_ref[...] * gate).astype(_BF)

        for i in range(n_chunks):
            jf, jb = i, n_chunks - 1 - i
            hf_ref[jf * T:(jf + 1) * T, :] = step(q_ref, k_ref, va_ref, gc_ref, gr_ref, jf * T, cf_ref, mf_ref, False)
            hb_ref[jb * T:(jb + 1) * T, :] = step(q_ref, k_ref, va_ref, gc_ref, gr_ref, jb * T, cb_ref, mb_ref, True)
            if jf >= jb:
                for ci in sorted({jf, jb}):
                    finalize(ci)

    run(qc_ref, kc_ref, vac_ref, oc_ref, gcc_ref, grc_ref, hfc_ref, hbc_ref, zc_ref)
    run(qx_ref, kx_ref, vax_ref, ox_ref, gcx_ref, grx_ref, hfx_ref, hbx_ref, zx_ref)


def _scan(batch, seq, ctx_len, d, qkvo_x, gcol_x, grow_x, qkvo_c, gcol_c, grow_c, hn_g, heads, dqk, chunk):
    dv = d // heads
    assert qkvo_x.shape[2] == 2 * dqk + 2 * dv
    assert seq % chunk == 0 and ctx_len % chunk == 0 and chunk % LANES == 0 and dv % LANES == 0

    def stream_specs(length):
        def blk(width):
            return pl.BlockSpec((None, length, width), lambda b, h: (h, b, 0))
        return [blk(2 * dqk + 2 * dv), blk(LANES), pl.BlockSpec((SUBLANES, length), lambda b, h: (h, b))]

    return pl.pallas_call(
        functools.partial(_scan_kernel, chunk=chunk, dqk=dqk, dv=dv),
        grid=(batch, heads),
        in_specs=stream_specs(seq) + stream_specs(ctx_len) + [pl.BlockSpec((1, dv), lambda b, h: (0, h))],
        out_specs=[
            pl.BlockSpec((seq, dv), lambda b, h: (b, h)),
            pl.BlockSpec((ctx_len, dv), lambda b, h: (b, h)),
        ],
        out_shape=[
            jax.ShapeDtypeStruct((batch * seq, d), _BF),
            jax.ShapeDtypeStruct((batch * ctx_len, d), _BF),
        ],
        scratch_shapes=[
            pltpu.VMEM((dqk, dv + LANES), _F32), pltpu.VMEM((SUBLANES, LANES), _F32),
            pltpu.VMEM((dqk, dv + LANES), _F32), pltpu.VMEM((SUBLANES, LANES), _F32),
            pltpu.VMEM((seq, dv + LANES), _BF), pltpu.VMEM((ctx_len, dv + LANES), _BF),
            pltpu.VMEM((seq, dv), _F32), pltpu.VMEM((seq, dv), _F32),
            pltpu.VMEM((ctx_len, dv), _F32), pltpu.VMEM((ctx_len, dv), _F32),
        ],
        compiler_params=_params("parallel", "parallel"),
        name="mlstm_scan",
    )(qkvo_x, gcol_x, grow_x, qkvo_c, gcol_c, grow_c, hn_g.reshape(1, d))


def kernel(x, c, ctx, c_ctx, norm1_g, norm2_g, w_mod, b_mod, w_gu, w_down, conv_w_pw1, conv_b_pw1,
           conv_w_dw, conv_b_dw, conv_ln_g, conv_ln_b, conv_w_pw2, conv_b_pw2, m_w_in, m_b_gates,
           m_hn_g, m_w_out, final_g):
    batch, seq, d = x.shape
    ctx_len = ctx.shape[1]
    depth = w_mod.shape[0]
    heads = max(4, d // 512)
    dqk_all = d // 2
    chunk = _tile(ctx_len, MXU_TILE)
    assert batch + 1 <= MOD_ROWS and 4 * heads <= LANES

    sx = _Stream(batch * seq, d, None)
    assert seq % sx.tm == 0
    tiles_per_sample = seq // sx.tm
    sx.mod_row = lambda i: i // tiles_per_sample
    sc = _Stream(batch * ctx_len, d, lambda i: batch)

    cc = jnp.zeros((MOD_ROWS, d), _F32).at[:batch].set(c).at[batch].set(c_ctx)
    mods = _modulation(cc, w_mod, b_mod).reshape(depth, MOD_ROWS, 6, 1, d)

    w_gu_b, w_down_b = w_gu.astype(_BF), w_down.astype(_BF)
    w_pw1_b, w_pw2_b = conv_w_pw1.astype(_BF), conv_w_pw2.astype(_BF)
    n_main = 2 * dqk_all + 2 * d
    w_in_b = m_w_in[:, :, :n_main].astype(_BF)
    w_gates_b = jnp.zeros((m_w_in.shape[0], d, LANES), _BF).at[:, :, :4 * heads].set(
        m_w_in[:, :, n_main:].astype(_BF))
    w_out_b = m_w_out.astype(_BF)

    xs = x.reshape(batch * seq, d)
    cs = ctx.reshape(batch * ctx_len, d)
    for i in range(depth):
        last = i == depth - 1
        j = i // 2
        ffn_w = (norm2_g[i], _Layered(w_gu_b, i), _Layered(w_down_b, i), final_g)
        if i % 2 == 0:
            conv_w = (conv_ln_g[j], conv_ln_b[j], _Layered(w_pw2_b, j), conv_b_pw2[j])
            w_pw1 = _Layered(w_pw1_b, j)
            ux = _pw1_glu(sx, xs, mods, i, norm1_g[i], w_pw1, conv_b_pw1[j])
            vx = _dwconv(ux, batch, seq, True, conv_w_dw[j], conv_b_dw[j])
            if not last:
                uc = _pw1_glu(sc, cs, mods, i, norm1_g[i], w_pw1, conv_b_pw1[j])
                vc = _dwconv(uc, batch, ctx_len, False, conv_w_dw[j], conv_b_dw[j])
                cs = _conv_out_ffn(sc, vc, cs, mods, i, *conv_w, *ffn_w, False)
            xs = _conv_out_ffn(sx, vx, xs, mods, i, *conv_w, *ffn_w, last)
        else:
            in_w = (_Layered(w_in_b, j), heads, dqk_all // heads, d // heads, _Layered(w_gates_b, j),
                    m_b_gates[j], chunk)
            w_out = _Layered(w_out_b, j)
            px, gcol_x, grow_x = _min_proj(sx, xs, mods, i, norm1_g[i], *in_w)
            pc, gcol_c, grow_c = _min_proj(sc, cs, mods, i, norm1_g[i], *in_w)
            zx, zc = _scan(batch, seq, ctx_len, d, px, gcol_x, grow_x, pc, gcol_c, grow_c,
                           m_hn_g[j], heads, dqk_all // heads, chunk)
            if not last:
                cs = _mlstm_out_ffn(sc, zc, cs, mods, i, w_out, *ffn_w, False)
            xs = _mlstm_out_ffn(sx, zx, xs, mods, i, w_out, *ffn_w, last)
    return xs.reshape(batch, seq, d)
```

```python
import functools

import jax
import jax.numpy as jnp
from jax import lax
from jax.experimental import pallas as pl
from jax.experimental.pallas import tpu as pltpu

EPS = 1e-6
GATE_CAP = 15.0
GRID_W = 64
LANES = 128
SUBLANES = 8
VMEM_LIMIT = 56 * 1024 * 1024
MOD_ROWS = 32
ROW_TILE = 512
COL_TILE = 512

_BF = jnp.bfloat16
_F32 = jnp.float32


def _tile(dim, pref):
    t = min(dim, pref)
    while dim % t:
        t //= 2
    return t


def _params(*sem):
    return pltpu.CompilerParams(dimension_semantics=sem, vmem_limit_bytes=VMEM_LIMIT)


def _sigmoid(v):
    return 0.5 * (1.0 + jnp.tanh(0.5 * v))


def _silu(v):
    h = 0.5 * v
    return h + h * jnp.tanh(h)


def _lane_tile(rep, width):
    if width % LANES == 0:
        return rep if width == LANES else jnp.concatenate([rep] * (width // LANES), axis=1)
    assert width < LANES
    return rep[:, :width]


def _dot(a, b):
    return jnp.dot(a, b, preferred_element_type=_F32)


def _rms_mod(x, g, shift, scale):
    y = x * lax.rsqrt(jnp.mean(x * x, axis=-1, keepdims=True) + EPS)
    return y * (g * (1.0 + scale)) + shift


def _resident(shape):
    zeros = (0,) * len(shape)
    return pl.BlockSpec(shape, lambda *_: zeros, pipeline_mode=pl.Buffered(1))


class _Layered:
    def __init__(self, stacked, index):
        self.arr, self.index = stacked, index
        self.shape = stacked.shape[1:]

    def resident(self, shape=None):
        shape = self.shape if shape is None else shape
        index = (self.index,) + (0,) * len(shape)
        return pl.BlockSpec((None,) + tuple(shape), lambda *_: index, pipeline_mode=pl.Buffered(1))

    def tiled(self, block, index_map):
        return pl.BlockSpec((None,) + tuple(block), lambda *a: (self.index,) + tuple(index_map(*a)))


def _mod_kernel(cc_ref, w_ref, b_ref, o_ref):
    cc = cc_ref[...]
    s = _silu(cc).astype(_BF)
    o_ref[...] = _dot(s, w_ref[...].astype(_BF)) + b_ref[...]


def _modulation(cc, w_mod, b_mod):
    depth, d, n = w_mod.shape
    tn = _tile(n, 1024)
    return pl.pallas_call(
        _mod_kernel,
        grid=(depth, n // tn),
        in_specs=[
            pl.BlockSpec((MOD_ROWS, d), lambda l, j: (0, 0)),
            pl.BlockSpec((None, d, tn), lambda l, j: (l, 0, j)),
            pl.BlockSpec((None, 1, tn), lambda l, j: (l, 0, j)),
        ],
        out_specs=pl.BlockSpec((None, MOD_ROWS, tn), lambda l, j: (l, 0, j)),
        out_shape=jax.ShapeDtypeStruct((depth, MOD_ROWS, n), _F32),
        compiler_params=_params("parallel", "parallel"),
        name="modulation",
    )(cc, w_mod, b_mod.reshape(depth, 1, n))


class _Stream:
    def __init__(self, n_rows, d, mod_row):
        self.n, self.d = n_rows, d
        self.tm = _tile(n_rows, ROW_TILE)
        self.tiles = n_rows // self.tm
        self.mod_row = mod_row

    def mod_spec(self, layer, part):
        return pl.BlockSpec((None, None, None, 1, self.d),
                            lambda i, *_: (layer, self.mod_row(i), part, 0, 0))


def _pw1_glu_kernel(x_ref, g_ref, sh_ref, sc_ref, w_ref, b_ref, u_ref, h_ref, *, tn):
    h_ref[...] = _rms_mod(x_ref[...], g_ref[...], sh_ref[...], sc_ref[...]).astype(_BF)
    d = h_ref.shape[1]
    for j in range(d // tn):
        lo, hi = j * tn, (j + 1) * tn
        a = _dot(h_ref[...], w_ref[:, lo:hi]) + b_ref[:, lo:hi]
        b = _dot(h_ref[...], w_ref[:, d + lo:d + hi]) + b_ref[:, d + lo:d + hi]
        u_ref[:, lo:hi] = (a * _sigmoid(b)).astype(u_ref.dtype)


def _pw1_glu(st, xs, mods, layer, g1, w_pw1, b_pw1):
    d, tm = st.d, st.tm
    return pl.pallas_call(
        functools.partial(_pw1_glu_kernel, tn=_tile(d, COL_TILE)),
        grid=(st.tiles,),
        in_specs=[
            pl.BlockSpec((tm, d), lambda i: (i, 0)),
            _resident((1, d)),
            st.mod_spec(layer, 0),
            st.mod_spec(layer, 1),
            w_pw1.resident(),
            _resident((1, 2 * d)),
        ],
        out_specs=pl.BlockSpec((tm, d), lambda i: (i, 0)),
        out_shape=jax.ShapeDtypeStruct((st.n, d), _BF),
        scratch_shapes=[pltpu.VMEM((tm, d), _BF)],
        compiler_params=_params("parallel"),
        name="pw1_glu",
    )(xs, g1.reshape(1, d), mods, mods, w_pw1.arr, b_pw1.reshape(1, 2 * d))


_CONV_LPAD = 16


def _conv_along_sublanes(pad_ref, w_ref, b_ref, n_rows, row_len, taps, store):
    tc = pad_ref.shape[-1]
    first = _CONV_LPAD - taps // 2

    def body(r, carry):
        for l0 in range(0, tc, LANES):
            lanes = pl.ds(l0, LANES)
            acc = jnp.zeros((row_len, LANES), _F32)
            for s in range(SUBLANES):
                ks = [k for k in range(taps) if (first + k) % SUBLANES == s]
                if not ks:
                    continue
                lo, hi = first + ks[0], first + ks[-1]
                n = row_len + hi - lo + (SUBLANES if s else 0)
                span = pad_ref[r, pl.ds(lo - s, n), lanes]
                if s:
                    span = pltpu.roll(span, n - s, 0)
                for k in ks:
                    off = first + k - lo
                    acc = acc + span[off:off + row_len] * w_ref[pl.ds(k, 1), lanes]
            store(r, l0, acc + b_ref[:, lanes])
        return carry
    lax.fori_loop(0, n_rows, body, 0)


def _dwconv_grid_kernel(u_ref, w_ref, b_ref, o_ref, padh_ref, padv_ref, *, n_h_tiles, taps):
    rows, _, tc = u_ref.shape
    half = taps // 2

    def store(r, l0, val):
        o_ref[r, :, pl.ds(l0, LANES)] = val.astype(o_ref.dtype)

    @pl.when(pl.program_id(1) < n_h_tiles)
    def _():
        zeros = jnp.zeros((rows, _CONV_LPAD, tc), _F32)
        padh_ref[:, pl.ds(0, _CONV_LPAD), :] = zeros
        padh_ref[:, pl.ds(_CONV_LPAD + GRID_W, _CONV_LPAD), :] = zeros
        padh_ref[:, pl.ds(_CONV_LPAD, GRID_W), :] = u_ref[...].astype(_F32)
        _conv_along_sublanes(padh_ref, w_ref, b_ref, rows, GRID_W, taps, store)

    @pl.when(pl.program_id(1) >= n_h_tiles)
    def _():
        zeros = jnp.zeros((half, GRID_W, tc), _F32)
        padv_ref[pl.ds(0, half)] = zeros
        padv_ref[pl.ds(half + rows, half)] = zeros
        padv_ref[pl.ds(half, rows)] = u_ref[...].astype(_F32)

        group = 2 if rows % 2 == 0 else 1

        def body(rg, carry):
            r0 = rg * group
            for l0 in range(0, tc, LANES):
                lanes = pl.ds(l0, LANES)
                accs = [jnp.zeros((GRID_W, LANES), _F32) for _ in range(group)]
                for k in range(taps + group - 1):
                    slab = padv_ref[r0 + k, :, lanes]
                    for g in range(group):
                        if 0 <= k - g < taps:
                            accs[g] = accs[g] + slab * w_ref[pl.ds(k - g, 1), lanes]
                for g in range(group):
                    store(r0 + g, l0, accs[g] + b_ref[:, lanes])
            return carry
        lax.fori_loop(0, rows // group, body, 0)


def _dwconv_seq_kernel(u_ref, w_ref, b_ref, o_ref, pad_ref, *, taps):
    n_seq, seq_len, tc = u_ref.shape
    zeros = jnp.zeros((n_seq, _CONV_LPAD, tc), _F32)
    pad_ref[:, pl.ds(0, _CONV_LPAD), :] = zeros
    pad_ref[:, pl.ds(_CONV_LPAD + seq_len, _CONV_LPAD), :] = zeros
    pad_ref[:, pl.ds(_CONV_LPAD, seq_len), :] = u_ref[...].astype(_F32)

    def store(r, l0, val):
        o_ref[r, :, pl.ds(l0, LANES)] = val.astype(o_ref.dtype)
    _conv_along_sublanes(pad_ref, w_ref, b_ref, n_seq, seq_len, taps, store)


def _dwconv(u, batch, seq, on_grid, w_dw, b_dw):
    n, d = u.shape
    taps = w_dw.shape[0]
    assert taps // 2 <= _CONV_LPAD
    tc = _tile(d // 2, 256)
    wspecs = [pl.BlockSpec((taps, tc), lambda i, c: (0, c)), pl.BlockSpec((1, tc), lambda i, c: (0, c))]
    if on_grid:
        assert seq % GRID_W == 0
        rows = seq // GRID_W
        block = pl.BlockSpec((rows, GRID_W, tc), lambda i, c: (i, 0, c))
        out = pl.pallas_call(
            functools.partial(_dwconv_grid_kernel, n_h_tiles=(d // 2) // tc, taps=taps),
            grid=(batch, d // tc),
            in_specs=[block] + wspecs,
            out_specs=block,
            out_shape=jax.ShapeDtypeStruct((n // GRID_W, GRID_W, d), u.dtype),
            scratch_shapes=[
                pltpu.VMEM((rows, GRID_W + 2 * _CONV_LPAD, tc), _F32),
                pltpu.VMEM((rows + 2 * (taps // 2), GRID_W, tc), _F32),
            ],
            compiler_params=_params("parallel", "parallel"),
            name="dwconv_grid",
        )(u.reshape(n // GRID_W, GRID_W, d), w_dw, b_dw.reshape(1, d))
    else:
        bb = _tile(batch, 8)
        block = pl.BlockSpec((bb, seq, tc), lambda i, c: (i, 0, c))
        out = pl.pallas_call(
            functools.partial(_dwconv_seq_kernel, taps=taps),
            grid=(batch // bb, d // tc),
            in_specs=[block] + wspecs,
            out_specs=block,
            out_shape=jax.ShapeDtypeStruct((batch, seq, d), u.dtype),
            scratch_shapes=[pltpu.VMEM((bb, seq + 2 * _CONV_LPAD, tc), _F32)],
            compiler_params=_params("parallel", "parallel"),
            name="dwconv_seq",
        )(u.reshape(batch, seq, d), w_dw, b_dw.reshape(1, d))
    return out.reshape(n, d)


def _ffn_step(h_ref, gate2_ref, wg_ref, wu_ref, wd_ref, fg_ref, o_ref, final_norm):
    h = h_ref[...]
    g = _dot(h, wg_ref[...])
    u = _dot(h, wu_ref[...])
    act = (_silu(g) * u).astype(_BF)
    o_ref[...] += gate2_ref[...] * _dot(act, wd_ref[...])
    if final_norm:
        @pl.when(pl.program_id(1) == pl.num_programs(1) - 1)
        def _():
            y = o_ref[...]
            o_ref[...] = y * lax.rsqrt(jnp.mean(y * y, axis=-1, keepdims=True) + EPS) * fg_ref[...]


def _conv_out_ffn_kernel(v_ref, lg_ref, lb_ref, w2_ref, b2_ref, x_ref, gate1_ref, g_ref, sh_ref, sc_ref,
                         gate2_ref, wg_ref, wu_ref, wd_ref, fg_ref, o_ref, h_ref, *, final_norm):
    @pl.when(pl.program_id(1) == 0)
    def _():
        v = v_ref[...].astype(_F32)
        mu = jnp.mean(v, axis=-1, keepdims=True)
        vc = v - mu
        var = jnp.mean(vc * vc, axis=-1, keepdims=True)
        y = vc * lax.rsqrt(var + EPS) * lg_ref[...] + lb_ref[...]
        z = _silu(y).astype(_BF)
        x1 = x_ref[...] + gate1_ref[...] * (_dot(z, w2_ref[...]) + b2_ref[...])
        o_ref[...] = x1
        h_ref[...] = _rms_mod(x1, g_ref[...], sh_ref[...], sc_ref[...]).astype(_BF)

    _ffn_step(h_ref, gate2_ref, wg_ref, wu_ref, wd_ref, fg_ref, o_ref, final_norm)


def _mlstm_out_ffn_kernel(z_ref, w2_ref, x_ref, gate1_ref, g_ref, sh_ref, sc_ref,
                          gate2_ref, wg_ref, wu_ref, wd_ref, fg_ref, o_ref, h_ref, *, final_norm):
    @pl.when(pl.program_id(1) == 0)
    def _():
        x1 = x_ref[...] + gate1_ref[...] * _dot(z_ref[...], w2_ref[...])
        o_ref[...] = x1
        h_ref[...] = _rms_mod(x1, g_ref[...], sh_ref[...], sc_ref[...]).astype(_BF)

    _ffn_step(h_ref, gate2_ref, wg_ref, wu_ref, wd_ref, fg_ref, o_ref, final_norm)


def _cast_kernel(w_ref, o_ref):
    o_ref[...] = w_ref[...].astype(o_ref.dtype)


def _ffn_gate_up_tiles(w_gu, tf):
    depth, d, two_f = w_gu.shape
    nf = two_f // 2 // tf
    return pl.pallas_call(
        _cast_kernel,
        grid=(depth, 2, nf),
        in_specs=[pl.BlockSpec((None, d, tf), lambda l, g, f: (l, 0, g * nf + f))],
        out_specs=pl.BlockSpec((None, None, None, d, tf), lambda l, g, f: (l, g, f, 0, 0)),
        out_shape=jax.ShapeDtypeStruct((depth, 2, nf, d, tf), _BF),
        compiler_params=_params("parallel", "parallel", "parallel"),
        name="ffn_weight_tiles",
    )(w_gu)


def _mix_out_ffn(st, mix_args, mix_specs, kern, xs, mods, layer, g2, w_gu, w_down, final_g, final_norm):
    d, tm = st.d, st.tm
    _, nf, _, tf = w_gu.shape
    assert w_down.shape == (nf * tf, d)
    row_block = pl.BlockSpec((tm, d), lambda i, f: (i, 0))
    return pl.pallas_call(
        functools.partial(kern, final_norm=final_norm),
        grid=(st.tiles, nf),
        in_specs=mix_specs + [
            row_block,
            st.mod_spec(layer, 2),
            _resident((1, d)),
            st.mod_spec(layer, 3),
            st.mod_spec(layer, 4),
            st.mod_spec(layer, 5),
            w_gu.tiled((None, None, d, tf), lambda i, f: (0, f, 0, 0)),
            w_gu.tiled((None, None, d, tf), lambda i, f: (1, f, 0, 0)),
            w_down.tiled((tf, d), lambda i, f: (f, 0)),
            _resident((1, d)),
        ],
        out_specs=row_block,
        out_shape=jax.ShapeDtypeStruct((st.n, d), _F32),
        scratch_shapes=[pltpu.VMEM((tm, d), _BF)],
        compiler_params=_params("parallel", "arbitrary"),
        name=kern.__name__.strip("_").replace("_kernel", ""),
    )(*mix_args, xs, mods, g2.reshape(1, d), mods, mods, mods, w_gu.arr, w_gu.arr, w_down.arr,
      final_g.reshape(1, d))


def _conv_out_ffn(st, v, xs, mods, layer, ln_g, ln_b, w_pw2, b_pw2, g2, w_gu, w_down, final_g, final_norm):
    d = st.d
    specs = [pl.BlockSpec((st.tm, d), lambda i, f: (i, 0)), _resident((1, d)), _resident((1, d)),
             w_pw2.resident(), _resident((1, d))]
    args = (v, ln_g.reshape(1, d), ln_b.reshape(1, d), w_pw2.arr, b_pw2.reshape(1, d))
    return _mix_out_ffn(st, args, specs, _conv_out_ffn_kernel, xs, mods, layer, g2, w_gu, w_down,
                        final_g, final_norm)


def _mlstm_out_ffn(st, z, xs, mods, layer, w_out, g2, w_gu, w_down, final_g, final_norm):
    d = st.d
    specs = [pl.BlockSpec((st.tm, d), lambda i, f: (i, 0)), w_out.resident()]
    return _mix_out_ffn(st, (z, w_out.arr), specs, _mlstm_out_ffn_kernel, xs, mods, layer, g2, w_gu, w_down,
                        final_g, final_norm)


def _log_sigmoid(v):
    return jnp.minimum(v, 0.0) - jnp.log(1.0 + jnp.exp(-jnp.abs(v)))


def _prepare_gates(g, col_ref, row_ref, heads, chunk):
    rows = g.shape[0]
    g = GATE_CAP * jnp.tanh(g / GATE_CAP)
    lane = lax.broadcasted_iota(jnp.int32, g.shape, 1)
    is_f = (lane // heads) % 2 == 1
    vals = jnp.where(is_f, _log_sigmoid(g), g)
    t = lax.broadcasted_iota(jnp.int32, g.shape, 0) % chunk
    pre = vals
    suf = vals
    shift = 1
    while shift < chunk:
        pre = pre + jnp.where(t >= shift, pltpu.roll(pre, shift, 0), 0.0)
        suf = suf + jnp.where(t + shift < chunk, pltpu.roll(suf, rows - shift, 0), 0.0)
        shift *= 2
    lane_o = lax.broadcasted_iota(jnp.int32, (rows, LANES), 1)
    for h in range(heads):
        li_f = vals[:, h:h + 1]
        b_f = pre[:, heads + h:heads + h + 1]
        li_b = vals[:, 2 * heads + h:2 * heads + h + 1]
        b_b = suf[:, 3 * heads + h:3 * heads + h + 1]
        blk = jnp.where(lane_o == 0, li_f, jnp.where(lane_o == 1, b_f, jnp.where(lane_o == 2, li_b, b_b)))
        col_ref[h] = blk
        row_ref[pl.ds(h * SUBLANES, SUBLANES), :] = blk.T[0:SUBLANES, :]


def _min_proj_kernel(x_ref, g_ref, sh_ref, sc_ref, w_ref, wg_ref, bg_ref, qkvo_ref,
                     col_ref, row_ref, h_ref, *, tn, chunk, dqk, dv):
    h_ref[...] = _rms_mod(x_ref[...], g_ref[...], sh_ref[...], sc_ref[...]).astype(_BF)
    heads = qkvo_ref.shape[0]
    _prepare_gates(_dot(h_ref[...], wg_ref[...]) + bg_ref[...], col_ref, row_ref, heads, chunk)
    k_scale = float(dqk) ** -0.5
    pieces, start, lane0 = [], 0, 0
    for width, scale in ((dqk, None), (dqk, k_scale), (dv, None), (dv, None)):
        for h in range(heads):
            pieces.append((start, width, lane0, h, scale))
            start += width
        lane0 += width
    for lo in range(0, start, tn):
        hi = lo + tn
        acc = _dot(h_ref[...], w_ref[:, lo:hi])
        for ps, width, lane0, h, scale in pieces:
            a, b = max(lo, ps), min(hi, ps + width)
            if a < b:
                part = acc[:, a - lo:b - lo]
                qkvo_ref[h, :, lane0 + a - ps:lane0 + b - ps] = (part if scale is None else part * scale).astype(_BF)


def _min_proj(st, xs, mods, layer, g1, w_in, heads, dqk, dv, w_gates, b_gates, chunk):
    d, tm = st.d, st.tm
    n_main = heads * (2 * dqk + 2 * dv)
    assert tm % chunk == 0 and 4 * heads <= LANES
    bias = jnp.zeros((1, LANES), _F32).at[0, :4 * heads].set(b_gates)

    def head_major(width, dtype):
        return (pl.BlockSpec((heads, tm, width), lambda i: (0, i, 0)),
                jax.ShapeDtypeStruct((heads, st.n, width), dtype))
    outs = [head_major(2 * dqk + 2 * dv, _BF),
            head_major(LANES, _F32),
            (pl.BlockSpec((heads * SUBLANES, tm), lambda i: (0, i)),
             jax.ShapeDtypeStruct((heads * SUBLANES, st.n), _F32))]
    return pl.pallas_call(
        functools.partial(_min_proj_kernel, tn=_tile(n_main, COL_TILE), chunk=chunk, dqk=dqk, dv=dv),
        grid=(st.tiles,),
        in_specs=[
            pl.BlockSpec((tm, d), lambda i: (i, 0)),
            _resident((1, d)),
            st.mod_spec(layer, 0),
            st.mod_spec(layer, 1),
            w_in.resident((d, n_main)),
            w_gates.resident(),
            _resident((1, LANES)),
        ],
        out_specs=[o[0] for o in outs],
        out_shape=[o[1] for o in outs],
        scratch_shapes=[pltpu.VMEM((tm, d), _BF)],
        compiler_params=_params("parallel"),
        name="mlstm_in_proj",
    )(xs, g1.reshape(1, d), mods, mods, w_in.arr, w_gates.arr, bias)


def _scan_kernel(sx_ref, gcx_ref, grx_ref, sc_ref, gcc_ref, grc_ref, hg_ref,
                 zx_ref, zc_ref,
                 cf_ref, mf_ref, cb_ref, mb_ref, vax_ref, vac_ref, hfx_ref, hbx_ref, hfc_ref, hbc_ref, *,
                 chunk, dqk, dv):
    T = chunk

    def split(s_ref):
        return (s_ref.at[:, pl.ds(0, dqk)], s_ref.at[:, pl.ds(dqk, dqk)],
                s_ref.at[:, pl.ds(2 * dqk, dv)], s_ref.at[:, pl.ds(2 * dqk + dv, dv)])
    qx_ref, kx_ref, vx_ref, ox_ref = split(sx_ref)
    qc_ref, kc_ref, vc_ref, oc_ref = split(sc_ref)
    row_i = lax.broadcasted_iota(jnp.int32, (T, T), 0)
    col_i = lax.broadcasted_iota(jnp.int32, (T, T), 1)

    for ref in (cf_ref, mf_ref, cb_ref, mb_ref):
        ref[...] = jnp.zeros_like(ref)
    for v_ref, va_ref in ((vx_ref, vax_ref), (vc_ref, vac_ref)):
        va_ref[:, :dv] = v_ref[...]
        va_ref[:, dv:] = jnp.ones((va_ref.shape[0], LANES), _BF)

    def step(q_ref, k_ref, va_ref, gc_ref, gr_ref, r0, c_ref, m_ref, backward):
        rows = slice(r0, r0 + T)
        q = q_ref[rows, :]
        k = k_ref[rows, :]
        va = va_ref[rows, :]
        gcb = gc_ref[rows, :]
        grb = gr_ref[:, rows]
        o = 2 if backward else 0
        li_rep = jnp.broadcast_to(gcb[:, o:o + 1], (T, LANES))
        b_rep = jnp.broadcast_to(gcb[:, o + 1:o + 2], (T, LANES))
        li_r, b_r = grb[o:o + 1, :], grb[o + 1:o + 2, :]
        m = m_ref[0:1, :]
        a_rep = b_rep + m
        mt_rep = jnp.maximum(a_rep, jnp.max(li_r, axis=1, keepdims=True))
        w_inter = jnp.exp(a_rep - mt_rep)
        d = _lane_tile(b_rep - mt_rep, T) + (li_r - b_r)
        p = jnp.exp(jnp.where((row_i <= col_i) if backward else (row_i >= col_i), d, -jnp.inf))
        qk = lax.dot_general(q, k, (((1,), (1,)), ((), ())), preferred_element_type=_F32)
        s = (qk * p).astype(_BF)
        c_old = c_ref[...]
        num = _lane_tile(w_inter, dv + LANES) * _dot(q, c_old.astype(_BF)) + _dot(s, va)
        den = num[:, dv:]
        inv = 1.0 / jnp.maximum(jnp.abs(den), jnp.exp(-mt_rep))
        h = num[:, :dv] * _lane_tile(inv, dv)
        b_tot = b_rep[0:1, :] if backward else b_rep[T - 1:T, :]
        gl = b_tot - b_rep + li_rep
        m_new = jnp.maximum(b_tot + m, jnp.max(gl, axis=0, keepdims=True))
        decay = jnp.exp(b_tot + m - m_new)
        kw = (k.astype(_F32) * _lane_tile(jnp.exp(gl - m_new), dqk)).astype(_BF)
        c_ref[...] = _lane_tile(decay, dv + LANES) * c_old + lax.dot_general(
            kw, va, (((0,), (0,)), ((), ())), preferred_element_type=_F32)
        m_ref[...] = jnp.broadcast_to(m_new, m_ref.shape)
        return h

    def run(q_ref, k_ref, va_ref, o_ref, gc_ref, gr_ref, hf_ref, hb_ref, z_ref):
        n_chunks = q_ref.shape[0] // T

        def finalize(ci):
            rows = slice(ci * T, (ci + 1) * T)
            hs = hf_ref[rows, :] + hb_ref[rows, :]
            hn = hs * lax.rsqrt(jnp.mean(hs * hs, axis=-1, keepdims=True) + EPS)
            gate = _sigmoid(o_ref[rows, :].astype(_F32))
            z_ref[rows, :] = (hn * hg_ref[...] * gate).astype(_BF)

        for i in range(n_chunks):
            jf, jb = i, n_chunks - 1 - i
            hf_ref[jf * T:(jf + 1) * T, :] = step(q_ref, k_ref, va_ref, gc_ref, gr_ref, jf * T, cf_ref, mf_ref, False)
            hb_ref[jb * T:(jb + 1) * T, :] = step(q_ref, k_ref, va_ref, gc_ref, gr_ref, jb * T, cb_ref, mb_ref, True)
            if jf >= jb:
                for ci in sorted({jf, jb}):
                    finalize(ci)

    run(qc_ref, kc_ref, vac_ref, oc_ref, gcc_ref, grc_ref, hfc_ref, hbc_ref, zc_ref)
    run(qx_ref, kx_ref, vax_ref, ox_ref, gcx_ref, grx_ref, hfx_ref, hbx_ref, zx_ref)


def _scan(batch, seq, ctx_len, d, qkvo_x, gcol_x, grow_x, qkvo_c, gcol_c, grow_c, hn_g, heads, dqk, chunk):
    dv = d // heads
    assert qkvo_x.shape[2] == 2 * dqk + 2 * dv
    assert seq % chunk == 0 and ctx_len % chunk == 0 and chunk % LANES == 0 and dv % LANES == 0

    def stream_specs(length):
        def blk(width):
            return pl.BlockSpec((None, length, width), lambda b, h: (h, b, 0))
        return [blk(2 * dqk + 2 * dv), blk(LANES), pl.BlockSpec((SUBLANES, length), lambda b, h: (h, b))]

    return pl.pallas_call(
        functools.partial(_scan_kernel, chunk=chunk, dqk=dqk, dv=dv),
        grid=(batch, heads),
        in_specs=stream_specs(seq) + stream_specs(ctx_len) + [pl.BlockSpec((1, dv), lambda b, h: (0, h))],
        out_specs=[
            pl.BlockSpec((seq, dv), lambda b, h: (b, h)),
            pl.BlockSpec((ctx_len, dv), lambda b, h: (b, h)),
        ],
        out_shape=[
            jax.ShapeDtypeStruct((batch * seq, d), _BF),
            jax.ShapeDtypeStruct((batch * ctx_len, d), _BF),
        ],
        scratch_shapes=[
            pltpu.VMEM((dqk, dv + LANES), _F32), pltpu.VMEM((SUBLANES, LANES), _F32),
            pltpu.VMEM((dqk, dv + LANES), _F32), pltpu.VMEM((SUBLANES, LANES), _F32),
            pltpu.VMEM((seq, dv + LANES), _BF), pltpu.VMEM((ctx_len, dv + LANES), _BF),
            pltpu.VMEM((seq, dv), _F32), pltpu.VMEM((seq, dv), _F32),
            pltpu.VMEM((ctx_len, dv), _F32), pltpu.VMEM((ctx_len, dv), _F32),
        ],
        compiler_params=_params("parallel", "parallel"),
        name="mlstm_scan",
    )(qkvo_x, gcol_x, grow_x, qkvo_c, gcol_c, grow_c, hn_g.reshape(1, d))


def kernel(x, c, ctx, c_ctx, norm1_g, norm2_g, w_mod, b_mod, w_gu, w_down, conv_w_pw1, conv_b_pw1,
           conv_w_dw, conv_b_dw, conv_ln_g, conv_ln_b, conv_w_pw2, conv_b_pw2, m_w_in, m_b_gates,
           m_hn_g, m_w_out, final_g):
    batch, seq, d = x.shape
    ctx_len = ctx.shape[1]
    depth = w_mod.shape[0]
    heads = max(4, d // 512)
    dqk_all = d // 2
    chunk = _tile(ctx_len, 256)
    assert batch + 1 <= MOD_ROWS and 4 * heads <= LANES

    sx = _Stream(batch * seq, d, None)
    assert seq % sx.tm == 0
    tiles_per_sample = seq // sx.tm
    sx.mod_row = lambda i: i // tiles_per_sample
    sc = _Stream(batch * ctx_len, d, lambda i: batch)

    cc = jnp.zeros((MOD_ROWS, d), _F32).at[:batch].set(c).at[batch].set(c_ctx)
    mods = _modulation(cc, w_mod, b_mod).reshape(depth, MOD_ROWS, 6, 1, d)

    w_gu_b = _ffn_gate_up_tiles(w_gu, _tile(w_down.shape[1], COL_TILE))
    w_down_b = w_down.astype(_BF)
    w_pw1_b, w_pw2_b = conv_w_pw1.astype(_BF), conv_w_pw2.astype(_BF)
    n_main = 2 * dqk_all + 2 * d
    w_in_b = m_w_in.astype(_BF)
    w_gates_b = jnp.zeros((m_w_in.shape[0], d, LANES), _BF).at[:, :, :4 * heads].set(
        m_w_in[:, :, n_main:].astype(_BF))
    w_out_b = m_w_out.astype(_BF)

    xs = x.reshape(batch * seq, d)
    cs = ctx.reshape(batch * ctx_len, d)
    for i in range(depth):
        last = i == depth - 1
        j = i // 2
        ffn_w = (norm2_g[i], _Layered(w_gu_b, i), _Layered(w_down_b, i), final_g)
        if i % 2 == 0:
            conv_w = (conv_ln_g[j], conv_ln_b[j], _Layered(w_pw2_b, j), conv_b_pw2[j])
            w_pw1 = _Layered(w_pw1_b, j)
            ux = _pw1_glu(sx, xs, mods, i, norm1_g[i], w_pw1, conv_b_pw1[j])
            vx = _dwconv(ux, batch, seq, True, conv_w_dw[j], conv_b_dw[j])
            if not last:
                uc = _pw1_glu(sc, cs, mods, i, norm1_g[i], w_pw1, conv_b_pw1[j])
                vc = _dwconv(uc, batch, ctx_len, False, conv_w_dw[j], conv_b_dw[j])
                cs = _conv_out_ffn(sc, vc, cs, mods, i, *conv_w, *ffn_w, False)
            xs = _conv_out_ffn(sx, vx, xs, mods, i, *conv_w, *ffn_w, last)
        else:
            in_w = (_Layered(w_in_b, j), heads, dqk_all // heads, d // heads, _Layered(w_gates_b, j),
                    m_b_gates[j], chunk)
            w_out = _Layered(w_out_b, j)
            px, gcol_x, grow_x = _min_proj(sx, xs, mods, i, norm1_g[i], *in_w)
            pc, gcol_c, grow_c = _min_proj(sc, cs, mods, i, norm1_g[i], *in_w)
            zx, zc = _scan(batch, seq, ctx_len, d, px, gcol_x, grow_x, pc, gcol_c, grow_c,
                           m_hn_g[j], heads, dqk_all // heads, chunk)
            if not last:
                cs = _mlstm_out_ffn(sc, zc, cs, mods, i, w_out, *ffn_w, False)
            xs = _mlstm_out_ffn(sx, zx, xs, mods, i, w_out, *ffn_w, last)
    return xs.reshape(batch, seq, d)
```

```python
import functools

import jax
import jax.numpy as jnp
from jax import lax
from jax.experimental import pallas as pl
from jax.experimental.pallas import tpu as pltpu

EPS = 1e-6
GATE_CAP = 15.0
GRID_W = 64
LANES = 128
SUBLANES = 8
VMEM_LIMIT = 56 * 1024 * 1024
MOD_ROWS = 32
ROW_TILE = 512
COL_TILE = 512
MXU_TILE = 256
MOD_COL_TILE = 1024
CTX_CONV_SEQS = 8

_BF = jnp.bfloat16
_F32 = jnp.float32


def _tile(dim, pref):
    t = min(dim, pref)
    while dim % t:
        t //= 2
    return t


def _params(*sem):
    return pltpu.CompilerParams(dimension_semantics=sem, vmem_limit_bytes=VMEM_LIMIT)


def _sigmoid(v):
    return 0.5 * (1.0 + jnp.tanh(0.5 * v))


def _silu(v):
    h = 0.5 * v
    return h + h * jnp.tanh(h)


def _lane_tile(rep, width):
    if width % LANES == 0:
        return rep if width == LANES else jnp.concatenate([rep] * (width // LANES), axis=1)
    assert width < LANES
    return rep[:, :width]


def _dot(a, b):
    return jnp.dot(a, b, preferred_element_type=_F32)


def _rms_mod(x, g, shift, scale):
    y = x * lax.rsqrt(jnp.mean(x * x, axis=-1, keepdims=True) + EPS)
    return y * (g * (1.0 + scale)) + shift


def _resident(shape):
    zeros = (0,) * len(shape)
    return pl.BlockSpec(shape, lambda *_: zeros, pipeline_mode=pl.Buffered(1))


class _Layered:
    def __init__(self, stacked, index):
        self.arr, self.index = stacked, index
        self.shape = stacked.shape[1:]

    def resident(self, shape=None):
        shape = self.shape if shape is None else shape
        index = (self.index,) + (0,) * len(shape)
        return pl.BlockSpec((None,) + tuple(shape), lambda *_: index, pipeline_mode=pl.Buffered(1))

    def tiled(self, block, index_map):
        return pl.BlockSpec((None,) + tuple(block), lambda *a: (self.index,) + tuple(index_map(*a)))


def _mod_kernel(cc_ref, w_ref, b_ref, o_ref):
    cc = cc_ref[...]
    s = _silu(cc).astype(_BF)
    o_ref[...] = _dot(s, w_ref[...].astype(_BF)) + b_ref[...]


def _modulation(cc, w_mod, b_mod):
    depth, d, n = w_mod.shape
    tn = _tile(n, MOD_COL_TILE)
    return pl.pallas_call(
        _mod_kernel,
        grid=(depth, n // tn),
        in_specs=[
            pl.BlockSpec((MOD_ROWS, d), lambda l, j: (0, 0)),
            pl.BlockSpec((None, d, tn), lambda l, j: (l, 0, j)),
            pl.BlockSpec((None, 1, tn), lambda l, j: (l, 0, j)),
        ],
        out_specs=pl.BlockSpec((None, MOD_ROWS, tn), lambda l, j: (l, 0, j)),
        out_shape=jax.ShapeDtypeStruct((depth, MOD_ROWS, n), _F32),
        compiler_params=_params("parallel", "parallel"),
        name="modulation",
    )(cc, w_mod, b_mod.reshape(depth, 1, n))


class _Stream:
    def __init__(self, n_rows, d, mod_row):
        self.n, self.d = n_rows, d
        self.tm = _tile(n_rows, ROW_TILE)
        self.tiles = n_rows // self.tm
        self.mod_row = mod_row

    def mod_spec(self, layer, part):
        return pl.BlockSpec((None, None, None, 1, self.d),
                            lambda i, *_: (layer, self.mod_row(i), part, 0, 0))


def _pw1_glu_kernel(x_ref, g_ref, sh_ref, sc_ref, w_ref, b_ref, u_ref, h_ref, *, tn):
    h_ref[...] = _rms_mod(x_ref[...], g_ref[...], sh_ref[...], sc_ref[...]).astype(_BF)
    d = h_ref.shape[1]
    for j in range(d // tn):
        lo, hi = j * tn, (j + 1) * tn
        a = _dot(h_ref[...], w_ref[:, lo:hi]) + b_ref[:, lo:hi]
        b = _dot(h_ref[...], w_ref[:, d + lo:d + hi]) + b_ref[:, d + lo:d + hi]
        u_ref[:, lo:hi] = (a * _sigmoid(b)).astype(u_ref.dtype)


def _pw1_glu(st, xs, mods, layer, g1, w_pw1, b_pw1):
    d, tm = st.d, st.tm
    return pl.pallas_call(
        functools.partial(_pw1_glu_kernel, tn=_tile(d, COL_TILE)),
        grid=(st.tiles,),
        in_specs=[
            pl.BlockSpec((tm, d), lambda i: (i, 0)),
            _resident((1, d)),
            st.mod_spec(layer, 0),
            st.mod_spec(layer, 1),
            w_pw1.resident(),
            _resident((1, 2 * d)),
        ],
        out_specs=pl.BlockSpec((tm, d), lambda i: (i, 0)),
        out_shape=jax.ShapeDtypeStruct((st.n, d), _BF),
        scratch_shapes=[pltpu.VMEM((tm, d), _BF)],
        compiler_params=_params("parallel"),
        name="pw1_glu",
    )(xs, g1.reshape(1, d), mods, mods, w_pw1.arr, b_pw1.reshape(1, 2 * d))


_CONV_LPAD = 16


def _conv_along_sublanes(pad_ref, w_ref, b_ref, n_rows, row_len, taps, store):
    tc = pad_ref.shape[-1]
    first = _CONV_LPAD - taps // 2

    def body(r, carry):
        for l0 in range(0, tc, LANES):
            lanes = pl.ds(l0, LANES)
            acc = jnp.broadcast_to(b_ref[:, lanes], (row_len, LANES))
            for s in range(SUBLANES):
                ks = [k for k in range(taps) if (first + k) % SUBLANES == s]
                if not ks:
                    continue
                lo, hi = first + ks[0], first + ks[-1]
                n = row_len + hi - lo + (SUBLANES if s else 0)
                span = pad_ref[r, pl.ds(lo - s, n), lanes]
                if s:
                    span = pltpu.roll(span, n - s, 0)
                for k in ks:
                    off = first + k - lo
                    acc = acc + span[off:off + row_len] * w_ref[pl.ds(k, 1), lanes]
            store(r, l0, acc)
        return carry
    lax.fori_loop(0, n_rows, body, 0)


def _dwconv_grid_kernel(u_ref, w_ref, b_ref, o_ref, padh_ref, padv_ref, *, n_h_tiles, taps):
    rows, _, tc = u_ref.shape
    half = taps // 2

    def store(r, l0, val):
        o_ref[r, :, pl.ds(l0, LANES)] = val.astype(o_ref.dtype)

    @pl.when(pl.program_id(1) < n_h_tiles)
    def _():
        zeros = jnp.zeros((rows, _CONV_LPAD, tc), _F32)
        padh_ref[:, pl.ds(0, _CONV_LPAD), :] = zeros
        padh_ref[:, pl.ds(_CONV_LPAD + GRID_W, _CONV_LPAD), :] = zeros
        padh_ref[:, pl.ds(_CONV_LPAD, GRID_W), :] = u_ref[...].astype(_F32)
        _conv_along_sublanes(padh_ref, w_ref, b_ref, rows, GRID_W, taps, store)

    @pl.when(pl.program_id(1) >= n_h_tiles)
    def _():
        zeros = jnp.zeros((half, GRID_W, tc), _F32)
        padv_ref[pl.ds(0, half)] = zeros
        padv_ref[pl.ds(half + rows, half)] = zeros
        padv_ref[pl.ds(half, rows)] = u_ref[...].astype(_F32)

        group = 2 if rows % 2 == 0 else 1

        def body(rg, carry):
            r0 = rg * group
            for l0 in range(0, tc, LANES):
                lanes = pl.ds(l0, LANES)
                accs = [jnp.broadcast_to(b_ref[:, lanes], (GRID_W, LANES))] * group
                for k in range(taps + group - 1):
                    slab = padv_ref[r0 + k, :, lanes]
                    for g in range(group):
                        if 0 <= k - g < taps:
                            accs[g] = accs[g] + slab * w_ref[pl.ds(k - g, 1), lanes]
                for g in range(group):
                    store(r0 + g, l0, accs[g])
            return carry
        lax.fori_loop(0, rows // group, body, 0)


def _dwconv_seq_kernel(u_ref, w_ref, b_ref, o_ref, pad_ref, *, taps):
    n_seq, seq_len, tc = u_ref.shape
    zeros = jnp.zeros((n_seq, _CONV_LPAD, tc), _F32)
    pad_ref[:, pl.ds(0, _CONV_LPAD), :] = zeros
    pad_ref[:, pl.ds(_CONV_LPAD + seq_len, _CONV_LPAD), :] = zeros
    pad_ref[:, pl.ds(_CONV_LPAD, seq_len), :] = u_ref[...].astype(_F32)

    def store(r, l0, val):
        o_ref[r, :, pl.ds(l0, LANES)] = val.astype(o_ref.dtype)
    _conv_along_sublanes(pad_ref, w_ref, b_ref, n_seq, seq_len, taps, store)


def _dwconv(u, batch, seq, on_grid, w_dw, b_dw):
    n, d = u.shape
    taps = w_dw.shape[0]
    assert taps // 2 <= _CONV_LPAD
    tc = _tile(d // 2, MXU_TILE)
    wspecs = [pl.BlockSpec((taps, tc), lambda i, c: (0, c)), pl.BlockSpec((1, tc), lambda i, c: (0, c))]
    if on_grid:
        assert seq % GRID_W == 0
        rows = seq // GRID_W
        block = pl.BlockSpec((rows, GRID_W, tc), lambda i, c: (i, 0, c))
        out = pl.pallas_call(
            functools.partial(_dwconv_grid_kernel, n_h_tiles=(d // 2) // tc, taps=taps),
            grid=(batch, d // tc),
            in_specs=[block] + wspecs,
            out_specs=block,
            out_shape=jax.ShapeDtypeStruct((n // GRID_W, GRID_W, d), u.dtype),
            scratch_shapes=[
                pltpu.VMEM((rows, GRID_W + 2 * _CONV_LPAD, tc), _F32),
                pltpu.VMEM((rows + 2 * (taps // 2), GRID_W, tc), _F32),
            ],
            compiler_params=_params("parallel", "parallel"),
            name="dwconv_grid",
        )(u.reshape(n // GRID_W, GRID_W, d), w_dw, b_dw.reshape(1, d))
    else:
        bb = _tile(batch, CTX_CONV_SEQS)
        block = pl.BlockSpec((bb, seq, tc), lambda i, c: (i, 0, c))
        out = pl.pallas_call(
            functools.partial(_dwconv_seq_kernel, taps=taps),
            grid=(batch // bb, d // tc),
            in_specs=[block] + wspecs,
            out_specs=block,
            out_shape=jax.ShapeDtypeStruct((batch, seq, d), u.dtype),
            scratch_shapes=[pltpu.VMEM((bb, seq + 2 * _CONV_LPAD, tc), _F32)],
            compiler_params=_params("parallel", "parallel"),
            name="dwconv_seq",
        )(u.reshape(batch, seq, d), w_dw, b_dw.reshape(1, d))
    return out.reshape(n, d)


def _ffn_step(h_ref, gate2_ref, wg_ref, wu_ref, wd_ref, fg_ref, o_ref, final_norm):
    h = h_ref[...]
    g = _dot(h, wg_ref[...])
    u = _dot(h, wu_ref[...])
    act = (_silu(g) * u).astype(_BF)
    o_ref[...] += gate2_ref[...] * _dot(act, wd_ref[...])
    if final_norm:
        @pl.when(pl.program_id(1) == pl.num_programs(1) - 1)
        def _():
            y = o_ref[...]
            o_ref[...] = y * lax.rsqrt(jnp.mean(y * y, axis=-1, keepdims=True) + EPS) * fg_ref[...]


def _conv_out_ffn_kernel(v_ref, lg_ref, lb_ref, w2_ref, b2_ref, x_ref, gate1_ref, g_ref, sh_ref, sc_ref,
                         gate2_ref, wg_ref, wu_ref, wd_ref, fg_ref, o_ref, h_ref, *, final_norm):
    @pl.when(pl.program_id(1) == 0)
    def _():
        v = v_ref[...].astype(_F32)
        mu = jnp.mean(v, axis=-1, keepdims=True)
        vc = v - mu
        var = jnp.mean(vc * vc, axis=-1, keepdims=True)
        y = vc * lax.rsqrt(var + EPS) * lg_ref[...] + lb_ref[...]
        z = _silu(y).astype(_BF)
        x1 = x_ref[...] + gate1_ref[...] * (_dot(z, w2_ref[...]) + b2_ref[...])
        o_ref[...] = x1
        h_ref[...] = _rms_mod(x1, g_ref[...], sh_ref[...], sc_ref[...]).astype(_BF)

    _ffn_step(h_ref, gate2_ref, wg_ref, wu_ref, wd_ref, fg_ref, o_ref, final_norm)


def _mlstm_out_ffn_kernel(z_ref, w2_ref, x_ref, gate1_ref, g_ref, sh_ref, sc_ref,
                          gate2_ref, wg_ref, wu_ref, wd_ref, fg_ref, o_ref, h_ref, *, final_norm):
    @pl.when(pl.program_id(1) == 0)
    def _():
        x1 = x_ref[...] + gate1_ref[...] * _dot(z_ref[...], w2_ref[...])
        o_ref[...] = x1
        h_ref[...] = _rms_mod(x1, g_ref[...], sh_ref[...], sc_ref[...]).astype(_BF)

    _ffn_step(h_ref, gate2_ref, wg_ref, wu_ref, wd_ref, fg_ref, o_ref, final_norm)


def _cast_kernel(w_ref, o_ref):
    o_ref[...] = w_ref[...].astype(o_ref.dtype)


def _ffn_gate_up_tiles(w_gu, tf):
    depth, d, two_f = w_gu.shape
    nf = two_f // 2 // tf
    return pl.pallas_call(
        _cast_kernel,
        grid=(depth, 2, nf),
        in_specs=[pl.BlockSpec((None, d, tf), lambda l, g, f: (l, 0, g * nf + f))],
        out_specs=pl.BlockSpec((None, None, None, d, tf), lambda l, g, f: (l, g, f, 0, 0)),
        out_shape=jax.ShapeDtypeStruct((depth, 2, nf, d, tf), _BF),
        compiler_params=_params("parallel", "parallel", "parallel"),
        name="ffn_weight_tiles",
    )(w_gu)


def _mix_out_ffn(st, mix_args, mix_specs, kern, xs, mods, layer, g2, w_gu, w_down, final_g, final_norm):
    d, tm = st.d, st.tm
    _, nf, _, tf = w_gu.shape
    assert w_down.shape == (nf * tf, d)
    row_block = pl.BlockSpec((tm, d), lambda i, f: (i, 0))
    return pl.pallas_call(
        functools.partial(kern, final_norm=final_norm),
        grid=(st.tiles, nf),
        in_specs=mix_specs + [
            row_block,
            st.mod_spec(layer, 2),
            _resident((1, d)),
            st.mod_spec(layer, 3),
            st.mod_spec(layer, 4),
            st.mod_spec(layer, 5),
            w_gu.tiled((None, None, d, tf), lambda i, f: (0, f, 0, 0)),
            w_gu.tiled((None, None, d, tf), lambda i, f: (1, f, 0, 0)),
            w_down.tiled((tf, d), lambda i, f: (f, 0)),
            _resident((1, d)),
        ],
        out_specs=row_block,
        out_shape=jax.ShapeDtypeStruct((st.n, d), _F32),
        scratch_shapes=[pltpu.VMEM((tm, d), _BF)],
        compiler_params=_params("parallel", "arbitrary"),
        name=kern.__name__.strip("_").replace("_kernel", ""),
    )(*mix_args, xs, mods, g2.reshape(1, d), mods, mods, mods, w_gu.arr, w_gu.arr, w_down.arr,
      final_g.reshape(1, d))


def _conv_out_ffn(st, v, xs, mods, layer, ln_g, ln_b, w_pw2, b_pw2, g2, w_gu, w_down, final_g, final_norm):
    d = st.d
    specs = [pl.BlockSpec((st.tm, d), lambda i, f: (i, 0)), _resident((1, d)), _resident((1, d)),
             w_pw2.resident(), _resident((1, d))]
    args = (v, ln_g.reshape(1, d), ln_b.reshape(1, d), w_pw2.arr, b_pw2.reshape(1, d))
    return _mix_out_ffn(st, args, specs, _conv_out_ffn_kernel, xs, mods, layer, g2, w_gu, w_down,
                        final_g, final_norm)


def _mlstm_out_ffn(st, z, xs, mods, layer, w_out, g2, w_gu, w_down, final_g, final_norm):
    d = st.d
    specs = [pl.BlockSpec((st.tm, d), lambda i, f: (i, 0)), w_out.resident()]
    return _mix_out_ffn(st, (z, w_out.arr), specs, _mlstm_out_ffn_kernel, xs, mods, layer, g2, w_gu, w_down,
                        final_g, final_norm)


def _log_sigmoid(v):
    return jnp.minimum(v, 0.0) - jnp.log(1.0 + jnp.exp(-jnp.abs(v)))


def _prepare_gates(g, col_ref, row_ref, heads, chunk):
    rows = g.shape[0]
    g = GATE_CAP * jnp.tanh(g / GATE_CAP)
    lane = lax.broadcasted_iota(jnp.int32, g.shape, 1)
    is_f = (lane // heads) % 2 == 1
    vals = jnp.where(is_f, _log_sigmoid(g), g)
    t = lax.broadcasted_iota(jnp.int32, g.shape, 0) % chunk
    pre = vals
    suf = vals
    shift = 1
    while shift < chunk:
        pre = pre + jnp.where(t >= shift, pltpu.roll(pre, shift, 0), 0.0)
        suf = suf + jnp.where(t + shift < chunk, pltpu.roll(suf, rows - shift, 0), 0.0)
        shift *= 2
    lane_o = lax.broadcasted_iota(jnp.int32, (rows, LANES), 1)
    for h in range(heads):
        li_f = vals[:, h:h + 1]
        b_f = pre[:, heads + h:heads + h + 1]
        li_b = vals[:, 2 * heads + h:2 * heads + h + 1]
        b_b = suf[:, 3 * heads + h:3 * heads + h + 1]
        blk = jnp.where(lane_o == 0, li_f, jnp.where(lane_o == 1, b_f, jnp.where(lane_o == 2, li_b, b_b)))
        col_ref[h] = blk
        row_ref[pl.ds(h * SUBLANES, SUBLANES), :] = blk.T[0:SUBLANES, :]


def _min_proj_kernel(x_ref, g_ref, sh_ref, sc_ref, w_ref, wg_ref, bg_ref, qkvo_ref,
                     col_ref, row_ref, h_ref, *, tn, chunk, dqk, dv):
    h_ref[...] = _rms_mod(x_ref[...], g_ref[...], sh_ref[...], sc_ref[...]).astype(_BF)
    heads = qkvo_ref.shape[0]
    _prepare_gates(_dot(h_ref[...], wg_ref[...]) + bg_ref[...], col_ref, row_ref, heads, chunk)
    k_scale = float(dqk) ** -0.5
    pieces, start, lane0 = [], 0, 0
    for width, scale in ((dqk, None), (dqk, k_scale), (dv, None), (dv, None)):
        for h in range(heads):
            pieces.append((start, width, lane0, h, scale))
            start += width
        lane0 += width
    for lo in range(0, start, tn):
        hi = lo + tn
        acc = _dot(h_ref[...], w_ref[:, lo:hi])
        for ps, width, lane0, h, scale in pieces:
            a, b = max(lo, ps), min(hi, ps + width)
            if a < b:
                part = acc[:, a - lo:b - lo]
                qkvo_ref[h, :, lane0 + a - ps:lane0 + b - ps] = (part if scale is None else part * scale).astype(_BF)


def _min_proj(st, xs, mods, layer, g1, w_in, heads, dqk, dv, w_gates, b_gates, chunk):
    d, tm = st.d, st.tm
    n_main = heads * (2 * dqk + 2 * dv)
    assert tm % chunk == 0 and 4 * heads <= LANES
    bias = jnp.zeros((1, LANES), _F32).at[0, :4 * heads].set(b_gates)

    def head_major(width, dtype):
        return (pl.BlockSpec((heads, tm, width), lambda i: (0, i, 0)),
                jax.ShapeDtypeStruct((heads, st.n, width), dtype))
    outs = [head_major(2 * dqk + 2 * dv, _BF),
            head_major(LANES, _F32),
            (pl.BlockSpec((heads * SUBLANES, tm), lambda i: (0, i)),
             jax.ShapeDtypeStruct((heads * SUBLANES, st.n), _F32))]
    return pl.pallas_call(
        functools.partial(_min_proj_kernel, tn=_tile(n_main, COL_TILE), chunk=chunk, dqk=dqk, dv=dv),
        grid=(st.tiles,),
        in_specs=[
            pl.BlockSpec((tm, d), lambda i: (i, 0)),
            _resident((1, d)),
            st.mod_spec(layer, 0),
            st.mod_spec(layer, 1),
            w_in.resident((d, n_main)),
            w_gates.resident(),
            _resident((1, LANES)),
        ],
        out_specs=[o[0] for o in outs],
        out_shape=[o[1] for o in outs],
        scratch_shapes=[pltpu.VMEM((tm, d), _BF)],
        compiler_params=_params("parallel"),
        name="mlstm_in_proj",
    )(xs, g1.reshape(1, d), mods, mods, w_in.arr, w_gates.arr, bias)


def _scan_kernel(sx_ref, gcx_ref, grx_ref, sc_ref, gcc_ref, grc_ref, hg_ref,
                 zx_ref, zc_ref,
                 cf_ref, mf_ref, cb_ref, mb_ref, vax_ref, vac_ref, hfx_ref, hbx_ref, hfc_ref, hbc_ref, *,
                 chunk, dqk, dv):
    T = chunk

    def split(s_ref):
        return (s_ref.at[:, pl.ds(0, dqk)], s_ref.at[:, pl.ds(dqk, dqk)],
                s_ref.at[:, pl.ds(2 * dqk, dv)], s_ref.at[:, pl.ds(2 * dqk + dv, dv)])
    qx_ref, kx_ref, vx_ref, ox_ref = split(sx_ref)
    qc_ref, kc_ref, vc_ref, oc_ref = split(sc_ref)
    row_i = lax.broadcasted_iota(jnp.int32, (T, T), 0)
    col_i = lax.broadcasted_iota(jnp.int32, (T, T), 1)

    for ref in (cf_ref, mf_ref, cb_ref, mb_ref):
        ref[...] = jnp.zeros_like(ref)
    for v_ref, va_ref in ((vx_ref, vax_ref), (vc_ref, vac_ref)):
        va_ref[:, :dv] = v_ref[...]
        va_ref[:, dv:] = jnp.ones((va_ref.shape[0], LANES), _BF)

    def step(q_ref, k_ref, va_ref, gc_ref, gr_ref, r0, c_ref, m_ref, backward):
        rows = slice(r0, r0 + T)
        q = q_ref[rows, :]
        k = k_ref[rows, :]
        va = va_ref[rows, :]
        gcb = gc_ref[rows, :]
        grb = gr_ref[:, rows]
        o = 2 if backward else 0
        li_rep = jnp.broadcast_to(gcb[:, o:o + 1], (T, LANES))
        b_rep = jnp.broadcast_to(gcb[:, o + 1:o + 2], (T, LANES))
        li_r, b_r = grb[o:o + 1, :], grb[o + 1:o + 2, :]
        m = m_ref[0:1, :]
        a_rep = b_rep + m
        mt_rep = jnp.maximum(a_rep, jnp.max(li_r, axis=1, keepdims=True))
        w_inter = jnp.exp(a_rep - mt_rep)
        d = _lane_tile(b_rep - mt_rep, T) + (li_r - b_r)
        p = jnp.exp(jnp.where((row_i <= col_i) if backward else (row_i >= col_i), d, -jnp.inf))
        qk = lax.dot_general(q, k, (((1,), (1,)), ((), ())), preferred_element_type=_F32)
        s = (qk * p).astype(_BF)
        c_old = c_ref[...]
        num = _lane_tile(w_inter, dv + LANES) * _dot(q, c_old.astype(_BF)) + _dot(s, va)
        den = num[:, dv:]
        inv = 1.0 / jnp.maximum(jnp.abs(den), jnp.exp(-mt_rep))
        h = num[:, :dv] * _lane_tile(inv, dv)
        b_tot = b_rep[0:1, :] if backward else b_rep[T - 1:T, :]
        gl = b_tot - b_rep + li_rep
        m_new = jnp.maximum(b_tot + m, jnp.max(gl, axis=0, keepdims=True))
        decay = jnp.exp(b_tot + m - m_new)
        kw = (k.astype(_F32) * _lane_tile(jnp.exp(gl - m_new), dqk)).astype(_BF)
        c_ref[...] = _lane_tile(decay, dv + LANES) * c_old + lax.dot_general(
            kw, va, (((0,), (0,)), ((), ())), preferred_element_type=_F32)
        m_ref[...] = jnp.broadcast_to(m_new, m_ref.shape)
        return h

    def run(q_ref, k_ref, va_ref, o_ref, gc_ref, gr_ref, hf_ref, hb_ref, z_ref):
        n_chunks = q_ref.shape[0] // T

        def finalize(ci):
            rows = slice(ci * T, (ci + 1) * T)
            hs = hf_ref[rows, :] + hb_ref[rows, :]
            hn = hs * lax.rsqrt(jnp.mean(hs * hs, axis=-1, keepdims=True) + EPS)
            gate = _sigmoid(o_ref[rows, :].astype(_F32))
            z_ref[rows, :] = (hn * hg_ref[...] * gate).astype(_BF)

        for i in range(n_chunks):
            jf, jb = i, n_chunks - 1 - i
            hf_ref[jf * T:(jf + 1) * T, :] = step(q_ref, k_ref, va_ref, gc_ref, gr_ref, jf * T, cf_ref, mf_ref, False)
            hb_ref[jb * T:(jb + 1) * T, :] = step(q_ref, k_ref, va_ref, gc_ref, gr_ref, jb * T, cb_ref, mb_ref, True)
            if jf >= jb:
                for ci in sorted({jf, jb}):
                    finalize(ci)

    run(qc_ref, kc_ref, vac_ref, oc_ref, gcc_ref, grc_ref, hfc_ref, hbc_ref, zc_ref)
    run(qx_ref, kx_ref, vax_ref, ox_ref, gcx_ref, grx_ref, hfx_ref, hbx_ref, zx_ref)


def _scan(batch, seq, ctx_len, d, qkvo_x, gcol_x, grow_x, qkvo_c, gcol_c, grow_c, hn_g, heads, dqk, chunk):
    dv = d // heads
    assert qkvo_x.shape[2] == 2 * dqk + 2 * dv
    assert seq % chunk == 0 and ctx_len % chunk == 0 and chunk % LANES == 0 and dv % LANES == 0

    def stream_specs(length):
        def blk(width):
            return pl.BlockSpec((None, length, width), lambda b, h: (h, b, 0))
        return [blk(2 * dqk + 2 * dv), blk(LANES), pl.BlockSpec((SUBLANES, length), lambda b, h: (h, b))]

    return pl.pallas_call(
        functools.partial(_scan_kernel, chunk=chunk, dqk=dqk, dv=dv),
        grid=(batch, heads),
        in_specs=stream_specs(seq) + stream_specs(ctx_len) + [pl.BlockSpec((1, dv), lambda b, h: (0, h))],
        out_specs=[
            pl.BlockSpec((seq, dv), lambda b, h: (b, h)),
            pl.BlockSpec((ctx_len, dv), lambda b, h: (b, h)),
        ],
        out_shape=[
            jax.ShapeDtypeStruct((batch * seq, d), _BF),
            jax.ShapeDtypeStruct((batch * ctx_len, d), _BF),
        ],
        scratch_shapes=[
            pltpu.VMEM((dqk, dv + LANES), _F32), pltpu.VMEM((SUBLANES, LANES), _F32),
            pltpu.VMEM((dqk, dv + LANES), _F32), pltpu.VMEM((SUBLANES, LANES), _F32),
            pltpu.VMEM((seq, dv + LANES), _BF), pltpu.VMEM((ctx_len, dv + LANES), _BF),
            pltpu.VMEM((seq, dv), _F32), pltpu.VMEM((seq, dv), _F32),
            pltpu.VMEM((ctx_len, dv), _F32), pltpu.VMEM((ctx_len, dv), _F32),
        ],
        compiler_params=_params("parallel", "parallel"),
        name="mlstm_scan",
    )(qkvo_x, gcol_x, grow_x, qkvo_c, gcol_c, grow_c, hn_g.reshape(1, d))


def kernel(x, c, ctx, c_ctx, norm1_g, norm2_g, w_mod, b_mod, w_gu, w_down, conv_w_pw1, conv_b_pw1,
           conv_w_dw, conv_b_dw, conv_ln_g, conv_ln_b, conv_w_pw2, conv_b_pw2, m_w_in, m_b_gates,
           m_hn_g, m_w_out, final_g):
    batch, seq, d = x.shape
    ctx_len = ctx.shape[1]
    depth = w_mod.shape[0]
    heads = max(4, d // 512)
    dqk_all = d // 2
    chunk = _tile(ctx_len, MXU_TILE)
    assert batch + 1 <= MOD_ROWS and 4 * heads <= LANES

    sx = _Stream(batch * seq, d, None)
    assert seq % sx.tm == 0
    tiles_per_sample = seq // sx.tm
    sx.mod_row = lambda i: i // tiles_per_sample
    sc = _Stream(batch * ctx_len, d, lambda i: batch)

    cc = jnp.zeros((MOD_ROWS, d), _F32).at[:batch].set(c).at[batch].set(c_ctx)
    mods = _modulation(cc, w_mod, b_mod).reshape(depth, MOD_ROWS, 6, 1, d)

    w_gu_b = _ffn_gate_up_tiles(w_gu, _tile(w_down.shape[1], COL_TILE))
    w_down_b = w_down.astype(_BF)
    w_pw1_b, w_pw2_b = conv_w_pw1.astype(_BF), conv_w_pw2.astype(_BF)
    n_main = 2 * dqk_all + 2 * d
    w_in_b = m_w_in[:, :, :n_main].astype(_BF)
    w_gates_b = jnp.zeros((m_w_in.shape[0], d, LANES), _BF).at[:, :, :4 * heads].set(
        m_w_in[:, :, n_main:].astype(_BF))
    w_out_b = m_w_out.astype(_BF)

    xs = x.reshape(batch * seq, d)
    cs = ctx.reshape(batch * ctx_len, d)
    for i in range(depth):
        last = i == depth - 1
        j = i // 2
        ffn_w = (norm2_g[i], _Layered(w_gu_b, i), _Layered(w_down_b, i), final_g)
        if i % 2 == 0:
            conv_w = (conv_ln_g[j], conv_ln_b[j], _Layered(w_pw2_b, j), conv_b_pw2[j])
            w_pw1 = _Layered(w_pw1_b, j)
            ux = _pw1_glu(sx, xs, mods, i, norm1_g[i], w_pw1, conv_b_pw1[j])
            vx = _dwconv(ux, batch, seq, True, conv_w_dw[j], conv_b_dw[j])
            if not last:
                uc = _pw1_glu(sc, cs, mods, i, norm1_g[i], w_pw1, conv_b_pw1[j])
                vc = _dwconv(uc, batch, ctx_len, False, conv_w_dw[j], conv_b_dw[j])
                cs = _conv_out_ffn(sc, vc, cs, mods, i, *conv_w, *ffn_w, False)
            xs = _conv_out_ffn(sx, vx, xs, mods, i, *conv_w, *ffn_w, last)
        else:
            in_w = (_Layered(w_in_b, j), heads, dqk_all // heads, d // heads, _Layered(w_gates_b, j),
                    m_b_gates[j], chunk)
            w_out = _Layered(w_out_b, j)
            px, gcol_x, grow_x = _min_proj(sx, xs, mods, i, norm1_g[i], *in_w)
            pc, gcol_c, grow_c = _min_proj(sc, cs, mods, i, norm1_g[i], *in_w)
            zx, zc = _scan(batch, seq, ctx_len, d, px, gcol_x, grow_x, pc, gcol_c, grow_c,
                           m_hn_g[j], heads, dqk_all // heads, chunk)
            if not last:
                cs = _mlstm_out_ffn(sc, zc, cs, mods, i, w_out, *ffn_w, False)
            xs = _mlstm_out_ffn(sx, zx, xs, mods, i, w_out, *ffn_w, last)
    return xs.reshape(batch, seq, d)
```
